```python
import jax, jax.numpy as jnp
from jax import lax
import numpy as np

D_MODEL = 2048
BATCH = 4
SEQ = 2048
DEPTH = 4

MIX_WIDTH = D_MODEL
N_MIXERS = 4
GW = MIX_WIDTH // N_MIXERS
HEAD_DIM = 128
HEADS_PER_GROUP = GW // HEAD_DIM
CHUNK = 128
POOL_WINDOWS = (2, 4, 8, 16)
POOL_CH = GW // len(POOL_WINDOWS)
CONV_WIDTH = 31
N_MEM = 256
MEM_HEADS = HEADS_PER_GROUP
N_EXPERTS = 64
N_EXPERT_GROUPS = 8
TOPK_GROUPS = 4
TOP_K = 8
D_EXPERT = 256
D_SHARED = 256
ROUTED_SCALE = 2.5
EXPERT_BLOCK = 128
LN_EPS = 1e-5
DEEPNORM_ALPHA = (2.0 * DEPTH) ** 0.25
DEEPNORM_BETA = (8.0 * DEPTH) ** -0.25
END_A = 2 * GW
END_B = 3 * GW
END_C = 5 * GW
IN_WIDTH = 6 * GW

kernel_name = "hybrid_sgu_pool_conv_memattn_moe_deepnorm"


def layer_norm(x, g, b):
    xf = x.astype(jnp.float32)
    mu = xf.mean(-1, keepdims=True)
    var = jnp.square(xf - mu).mean(-1, keepdims=True)
    y = (xf - mu) * lax.rsqrt(var + LN_EPS)
    return (y * g.astype(jnp.float32) + b.astype(jnp.float32)).astype(x.dtype)


def group_norm(x, g, b, n_groups):
    shp = x.shape
    xg = x.reshape(shp[:-1] + (n_groups, shp[-1] // n_groups))
    return layer_norm(xg, g.reshape(n_groups, -1), b.reshape(n_groups, -1)).reshape(shp)


def sgu_mixer(h, ln_g, ln_b, w_s, b_s):
    z = jax.nn.gelu(h)
    u, v = z[..., :GW], z[..., GW:]
    v = group_norm(v, ln_g, ln_b, HEADS_PER_GROUP)
    B, S, _ = v.shape
    vc = v.reshape(B, S // CHUNK, CHUNK, HEADS_PER_GROUP, HEAD_DIM)
    causal = jnp.tril(jnp.ones((CHUNK, CHUNK), dtype=bool))
    wm = jnp.where(causal[None], w_s, 0)
    mixed = jnp.einsum('hts,bcshd->bcthd', wm, vc) + b_s[:, :, None]
    return u * mixed.reshape(B, S, GW)


def pool_mixer(p, w_pool, scale):
    B, S, _ = p.shape
    pg = p.reshape(B, S, len(POOL_WINDOWS), POOL_CH)
    pos = jnp.arange(1, S + 1, dtype=jnp.float32)
    outs = []
    for g, win in enumerate(POOL_WINDOWS):
        xg = pg[:, :, g].astype(jnp.float32)
        cs = lax.cumsum(xg, axis=1)
        lagged = jnp.pad(cs, ((0, 0), (win, 0), (0, 0)))[:, :S]
        cnt = jnp.minimum(pos, float(win))[None, :, None]
        outs.append((cs - lagged) / cnt - xg)
    pooled = jnp.stack(outs, axis=2).astype(p.dtype)
    y = jnp.einsum('bsgc,gcd->bsgd', pooled, w_pool).reshape(B, S, GW)
    return y * scale


def conv_mixer(c, w_conv, b_conv, ln_g, ln_b):
    glu = c[..., :GW] * jax.nn.sigmoid(c[..., GW:])
    y = lax.conv_general_dilated(
        glu, w_conv[:, None, :], window_strides=(1,), padding=[(CONV_WIDTH - 1, 0)],
        dimension_numbers=('NWC', 'WIO', 'NWC'), feature_group_count=GW) + b_conv
    y = group_norm(y, ln_g, ln_b, HEADS_PER_GROUP)
    return jax.nn.silu(y)


def memory_attention(q, memn, w_kv):
    B, S, _ = q.shape
    M = memn.shape[1]
    kv = memn @ w_kv
    k = kv[..., :GW].reshape(B, M, MEM_HEADS, HEAD_DIM)
    v = kv[..., GW:].reshape(B, M, MEM_HEADS, HEAD_DIM)
    qh = q.reshape(B, S, MEM_HEADS, HEAD_DIM)
    s = jnp.einsum('bshd,bmhd->bhsm', qh.astype(jnp.float32), k.astype(jnp.float32)) * (HEAD_DIM ** -0.5)
    pr = jax.nn.softmax(s, axis=-1)
    o = jnp.einsum('bhsm,bmhd->bshd', pr, v.astype(jnp.float32))
    return o.reshape(B, S, GW).astype(q.dtype)


def route(xf, w_router, bias):
    N = xf.shape[0]
    scores = jax.nn.sigmoid(xf.astype(jnp.float32) @ w_router.astype(jnp.float32))
    choice = scores + bias.astype(jnp.float32)
    grp = choice.reshape(N, N_EXPERT_GROUPS, N_EXPERTS // N_EXPERT_GROUPS)
    grp_score = lax.top_k(grp, 2)[0].sum(-1)
    _, top_g = lax.top_k(grp_score, TOPK_GROUPS)
    gmask = jax.nn.one_hot(top_g, N_EXPERT_GROUPS, dtype=jnp.float32).sum(1) > 0
    emask = jnp.repeat(gmask, N_EXPERTS // N_EXPERT_GROUPS, axis=1)
    _, idx = lax.top_k(jnp.where(emask, choice, -jnp.inf), TOP_K)
    w = jnp.take_along_axis(scores, idx, axis=-1)
    w = w / (w.sum(-1, keepdims=True) + 1e-20) * ROUTED_SCALE
    return idx, w


def swiglu(x, wg, wu, wd):
    return (jax.nn.silu(x @ wg) * (x @ wu)) @ wd


def routed_experts(xf, idx, wts, w_gate, w_up, w_down):
    N, D = xf.shape
    A = N * TOP_K
    flat_e = idx.reshape(A)
    flat_tok = jnp.repeat(jnp.arange(N, dtype=jnp.int32), TOP_K)
    flat_w = wts.reshape(A)
    order = jnp.argsort(flat_e)
    e_sorted = flat_e[order]
    counts = jnp.bincount(flat_e, length=N_EXPERTS)
    padded = (counts + EXPERT_BLOCK - 1) // EXPERT_BLOCK * EXPERT_BLOCK
    pad_end = jnp.cumsum(padded)
    pad_start = pad_end - padded
    start = jnp.cumsum(counts) - counts
    dest = pad_start[e_sorted] + (jnp.arange(A, dtype=jnp.int32) - start[e_sorted])
    nb = -(-A // EXPERT_BLOCK) + N_EXPERTS
    slot_tok = jnp.full((nb * EXPERT_BLOCK,), N, jnp.int32).at[dest].set(flat_tok[order])
    slot_w = jnp.zeros((nb * EXPERT_BLOCK,), jnp.float32).at[dest].set(flat_w[order])
    blk_e = jnp.minimum(jnp.searchsorted(pad_end, jnp.arange(nb) * EXPERT_BLOCK, side='right'), N_EXPERTS - 1)
    x_pad = jnp.concatenate([xf, jnp.zeros((1, D), xf.dtype)], axis=0)

    def body(y, blk):
        tok, w, e = blk
        xb = x_pad[tok]
        yb = swiglu(xb, w_gate[e], w_up[e], w_down[e]) * w.astype(xf.dtype)[:, None]
        return y.at[tok].add(yb), None

    y, _ = lax.scan(body, jnp.zeros((N + 1, D), xf.dtype),
                    (slot_tok.reshape(nb, EXPERT_BLOCK), slot_w.reshape(nb, EXPERT_BLOCK), blk_e))
    return y[:N]


def setup_inputs(seed: int = 0) -> dict:
    key = jax.random.key(seed)
    ks = jax.random.split(key, 40)
    f32 = jnp.float32

    def nrm(k, shape, scale):
        return jax.random.normal(k, shape, f32) * scale

    L, D, H = DEPTH, D_MODEL, HEADS_PER_GROUP
    w_mem_kv = jnp.concatenate([nrm(ks[14], (L, D, GW), D ** -0.5),
                                nrm(ks[15], (L, D, GW), D ** -0.5 * DEEPNORM_BETA)], axis=-1)
    return {
        'x': nrm(ks[0], (BATCH, SEQ, D), 1.0),
        'mem': nrm(ks[1], (BATCH, N_MEM, D), 1.0),
        'mem_ln_g': 1.0 + nrm(ks[2], (D,), 0.01),
        'mem_ln_b': nrm(ks[3], (D,), 0.01),
        'w_in': nrm(ks[4], (L, D, IN_WIDTH), D ** -0.5),
        'sgu_ln_g': 1.0 + nrm(ks[5], (L, GW), 0.01),
        'sgu_ln_b': nrm(ks[6], (L, GW), 0.01),
        'sgu_w': nrm(ks[7], (L, H, CHUNK, CHUNK), CHUNK ** -0.5),
        'sgu_b': 1.0 + nrm(ks[8], (L, CHUNK, H), 0.01),
        'pool_w': nrm(ks[9], (L, len(POOL_WINDOWS), POOL_CH, POOL_CH), POOL_CH ** -0.5),
        'pool_scale': 1.0 + nrm(ks[10], (L, GW), 0.01),
        'conv_w': nrm(ks[11], (L, CONV_WIDTH, GW), CONV_WIDTH ** -0.5),
        'conv_b': nrm(ks[12], (L, GW), 0.01),
        'conv_ln_g': 1.0 + nrm(ks[13], (L, GW), 0.01),
        'conv_ln_b': nrm(ks[16], (L, GW), 0.01),
        'w_mem_kv': w_mem_kv,
        'w_out': nrm(ks[17], (L, MIX_WIDTH, D), MIX_WIDTH ** -0.5 * DEEPNORM_BETA),
        'ln1_g': 1.0 + nrm(ks[18], (L, D), 0.01),
        'ln1_b': nrm(ks[19], (L, D), 0.01),
        'w_router': nrm(ks[20], (L, D, N_EXPERTS), D ** -0.5),
        'router_bias': nrm(ks[21], (L, N_EXPERTS), 0.01),
        'exp_w_gate': nrm(ks[22], (L, N_EXPERTS, D, D_EXPERT), D ** -0.5),
        'exp_w_up': nrm(ks[23], (L, N_EXPERTS, D, D_EXPERT), D ** -0.5),
        'exp_w_down': nrm(ks[24], (L, N_EXPERTS, D_EXPERT, D), D_EXPERT ** -0.5 * DEEPNORM_BETA),
        'sh_w_gate': nrm(ks[25], (L, D, D_SHARED), D ** -0.5),
        'sh_w_up': nrm(ks[26], (L, D, D_SHARED), D ** -0.5),
        'sh_w_down': nrm(ks[27], (L, D_SHARED, D), D_SHARED ** -0.5 * DEEPNORM_BETA),
        'ln2_g': 1.0 + nrm(ks[28], (L, D), 0.01),
        'ln2_b': nrm(ks[29], (L, D), 0.01),
    }


def reference(x, mem, mem_ln_g, mem_ln_b, w_in, sgu_ln_g, sgu_ln_b, sgu_w, sgu_b, pool_w, pool_scale,
              conv_w, conv_b, conv_ln_g, conv_ln_b, w_mem_kv, w_out, ln1_g, ln1_b, w_router, router_bias,
              exp_w_gate, exp_w_up, exp_w_down, sh_w_gate, sh_w_up, sh_w_down, ln2_g, ln2_b):
    B, S, D = x.shape
    memn = layer_norm(mem, mem_ln_g, mem_ln_b)
    for l in range(DEPTH):
        h = x @ w_in[l]
        a = sgu_mixer(h[..., :END_A], sgu_ln_g[l], sgu_ln_b[l], sgu_w[l], sgu_b[l])
        p = pool_mixer(h[..., END_A:END_B], pool_w[l], pool_scale[l])
        c = conv_mixer(h[..., END_B:END_C], conv_w[l], conv_b[l], conv_ln_g[l], conv_ln_b[l])
        m = memory_attention(h[..., END_C:], memn, w_mem_kv[l])
        mix = jnp.concatenate([a, p, c, m], axis=-1) @ w_out[l]
        x = layer_norm(DEEPNORM_ALPHA * x + mix, ln1_g[l], ln1_b[l])
        xf = x.reshape(B * S, D)
        idx, wts = route(xf, w_router[l], router_bias[l])
        shared = swiglu(xf, sh_w_gate[l], sh_w_up[l], sh_w_down[l])
        routed = routed_experts(xf, idx, wts, exp_w_gate[l], exp_w_up[l], exp_w_down[l])
        x = layer_norm(DEEPNORM_ALPHA * x + (shared + routed).reshape(B, S, D), ln2_g[l], ln2_b[l])
    return x
```

```python
import functools

import jax
import jax.numpy as jnp
from jax import lax
from jax.experimental import pallas as pl
from jax.experimental.pallas import tpu as pltpu

D_MODEL = 2048
DEPTH = 4
GW = 512
HEAD_DIM = 128
HEADS = 4
CHUNK = 128
POOL_WINDOWS = (2, 4, 8, 16)
CONV_WIDTH = 31
N_MEM = 256
N_EXPERTS = 64
N_GROUPS = 8
GROUP_SIZE = N_EXPERTS // N_GROUPS
TOPK_GROUPS = 4
TOP_K = 8
D_EXPERT = 256
D_SHARED = 256
ROUTED_SCALE = 2.5
LN_EPS = 1e-5
ALPHA = (2.0 * DEPTH) ** 0.25

LANES = 128
SUBLANES = 8
HALF = D_MODEL // 2
HALO = 32
VMEM_LIMIT = 56 * 1024 * 1024

TS_MIX = 256
BM_IN = 512
BN_IN = 1024
T_ROUTE = 512
BM_EXP = 128
TS_FIN = 256

F32 = jnp.float32
BF16 = jnp.bfloat16
U32 = jnp.uint32
I32 = jnp.int32
HI_MASK = 0xFFFF0000


def _ln(x, g, b):
    mu = jnp.mean(x, axis=-1, keepdims=True)
    xc = x - mu
    var = jnp.mean(xc * xc, axis=-1, keepdims=True)
    return xc * lax.rsqrt(var + LN_EPS) * g + b


def _dot(a, b):
    return jnp.dot(a, b, preferred_element_type=F32)


def _dot_nt(a, b):
    return lax.dot_general(a, b, (((1,), (1,)), ((), ())), preferred_element_type=F32)


def _kv_kernel(mem_ref, g_ref, b_ref, w_ref, o_ref):
    memn = _ln(mem_ref[...], g_ref[...], b_ref[...])
    o_ref[...] = _dot(memn.astype(BF16), w_ref[...].astype(BF16)).astype(BF16)


def _memory_kv(mem2, g, b, w_mem_kv):
    nb = mem2.shape[0] // N_MEM
    return pl.pallas_call(
        _kv_kernel,
        grid=(DEPTH, nb),
        in_specs=[
            pl.BlockSpec((N_MEM, D_MODEL), lambda l, i: (i, 0)),
            pl.BlockSpec((1, D_MODEL), lambda l, i: (0, 0)),
            pl.BlockSpec((1, D_MODEL), lambda l, i: (0, 0)),
            pl.BlockSpec((None, D_MODEL, 2 * GW), lambda l, i: (l, 0, 0)),
        ],
        out_specs=pl.BlockSpec((None, N_MEM, 2 * GW), lambda l, i: (l, i, 0)),
        out_shape=jax.ShapeDtypeStruct((DEPTH, mem2.shape[0], 2 * GW), BF16),
        compiler_params=pltpu.CompilerParams(vmem_limit_bytes=VMEM_LIMIT),
        name="memory_kv",
    )(mem2, g, b, w_mem_kv)


def _inproj_kernel(x_ref, w_ref, o_ref):
    acc = _dot(x_ref[...], w_ref[...])
    o_ref[0] = acc[:, :GW]
    o_ref[1] = acc[:, GW:]


def _in_proj(xbf, w_in_p, l):
    n = xbf.shape[0]
    return pl.pallas_call(
        _inproj_kernel,
        grid=(6 * GW // BN_IN, n // BM_IN),
        in_specs=[
            pl.BlockSpec((BM_IN, D_MODEL), lambda j, i: (i, 0)),
            pl.BlockSpec((None, D_MODEL, BN_IN), lambda j, i: (l, 0, j)),
        ],
        out_specs=pl.BlockSpec((BN_IN // GW, BM_IN, GW), lambda j, i: (j, i, 0)),
        out_shape=jax.ShapeDtypeStruct((6, n, GW), F32),
        compiler_params=pltpu.CompilerParams(vmem_limit_bytes=VMEM_LIMIT),
        name="in_proj",
    )(xbf, w_in_p)


def _pack_tokens(xb, xpk_ref):
    rows = xb.shape[0]
    bits = lax.bitcast_convert_type(xb.astype(F32), U32)
    for s in range(SUBLANES):
        lo = bits[:, s * LANES:(s + 1) * LANES] >> 16
        hi = bits[:, HALF + s * LANES:HALF + (s + 1) * LANES] & jnp.uint32(HI_MASK)
        xpk_ref[pl.ds(s, rows, stride=SUBLANES), :] = hi | lo


def _mix_kernel(h_ref, halo_ref, kv_ref, x_ref, wout_ref, sguw_ref, sgub_ref, sgug_ref, sgubeta_ref,
                poolw_ref, pools_ref, convw_ref, convb_ref, cvg_ref, cvb_ref, ln1g_ref, ln1b_ref,
                x1_ref, x1bf_ref, xpk_ref, mix_ref, cbuf_ref):
    i = pl.program_id(1)
    ts = h_ref.shape[1]
    first = i == 0

    u = jax.nn.gelu(h_ref[0])
    v = jax.nn.gelu(h_ref[1])
    tri = (lax.broadcasted_iota(I32, (CHUNK, CHUNK), 0) >= lax.broadcasted_iota(I32, (CHUNK, CHUNK), 1))
    for hd in range(HEADS):
        cs = slice(hd * HEAD_DIM, (hd + 1) * HEAD_DIM)
        vn = _ln(v[:, cs], sgug_ref[:, cs], sgubeta_ref[:, cs]).astype(BF16)
        wm = jnp.where(tri, sguw_ref[hd], 0.0).astype(BF16)
        bcol = sgub_ref[:, hd:hd + 1]
        for c in range(ts // CHUNK):
            rs = slice(c * CHUNK, (c + 1) * CHUNK)
            mixed = _dot(wm, vn[rs]) + bcol
            mix_ref[rs, cs] = (u[rs, cs] * mixed).astype(BF16)

    hp = h_ref[3]
    halo_p = jnp.where(first, 0.0, halo_ref[0])
    ext = jnp.concatenate([halo_p, hp], axis=0)
    pos1 = (i * ts + lax.broadcasted_iota(I32, (ts, 1), 0) + 1).astype(F32)
    for g, win in enumerate(POOL_WINDOWS):
        cs = slice(g * LANES, (g + 1) * LANES)
        s = ext[:, cs]
        sh = 1
        while sh < win:
            s = s + pltpu.roll(s, sh, 0)
            sh *= 2
        cnt = jnp.minimum(pos1, float(win))
        pooled = s[HALO:] / cnt - hp[:, cs]
        y = _dot(pooled.astype(BF16), poolw_ref[g].astype(BF16)) * pools_ref[:, cs]
        mix_ref[:, GW + g * LANES:GW + (g + 1) * LANES] = y.astype(BF16)

    cbuf_ref[0:HALO, :] = jnp.where(first, 0.0, halo_ref[1] * jax.nn.sigmoid(halo_ref[2]))
    cbuf_ref[HALO:, :] = h_ref[4] * jax.nn.sigmoid(h_ref[5])
    rchunk = 64
    for g in range(GW // LANES):
        cs = slice(g * LANES, (g + 1) * LANES)
        parts = []
        for r0 in range(0, ts, rchunk):
            acc = jnp.zeros((rchunk, LANES), F32)
            for j in range(CONV_WIDTH):
                start = HALO - (CONV_WIDTH - 1) + j + r0
                acc = acc + convw_ref[j:j + 1, cs] * cbuf_ref[start:start + rchunk, cs]
            parts.append(acc)
        y = jnp.concatenate(parts, axis=0) + convb_ref[:, cs]
        y = _ln(y, cvg_ref[:, cs], cvb_ref[:, cs])
        mix_ref[:, 2 * GW + g * LANES:2 * GW + (g + 1) * LANES] = jax.nn.silu(y).astype(BF16)

    q = h_ref[2]
    for hd in range(HEADS):
        cs = slice(hd * HEAD_DIM, (hd + 1) * HEAD_DIM)
        kh = kv_ref[:, hd * HEAD_DIM:(hd + 1) * HEAD_DIM]
        vh = kv_ref[:, GW + hd * HEAD_DIM:GW + (hd + 1) * HEAD_DIM]
        sc = _dot_nt(q[:, cs].astype(BF16), kh) * (HEAD_DIM ** -0.5)
        e = jnp.exp(sc - jnp.max(sc, axis=-1, keepdims=True))
        pr = e / jnp.sum(e, axis=-1, keepdims=True)
        mix_ref[:, 3 * GW + hd * HEAD_DIM:3 * GW + (hd + 1) * HEAD_DIM] = _dot(pr.astype(BF16), vh).astype(BF16)

    z = ALPHA * x_ref[...] + _dot(mix_ref[...], wout_ref[...])
    x1 = _ln(z, ln1g_ref[...], ln1b_ref[...])
    x1_ref[...] = x1
    xb = x1.astype(BF16)
    x1bf_ref[...] = xb
    _pack_tokens(xb, xpk_ref)


def _mixers(h6, kv, x, wout_bf, p, l, batch, seq):
    n = x.shape[0]
    ns = seq // TS_MIX
    hb = TS_MIX // HALO

    def vec(width):
        return pl.BlockSpec((None, 1, width), lambda b, i: (l, 0, 0))

    return pl.pallas_call(
        _mix_kernel,
        grid=(batch, ns),
        in_specs=[
            pl.BlockSpec((6, TS_MIX, GW), lambda b, i: (0, b * ns + i, 0)),
            pl.BlockSpec((3, HALO, GW), lambda b, i: (1, jnp.maximum((b * ns + i) * hb - 1, 0), 0)),
            pl.BlockSpec((None, N_MEM, 2 * GW), lambda b, i: (l, b, 0)),
            pl.BlockSpec((TS_MIX, D_MODEL), lambda b, i: (b * ns + i, 0)),
            pl.BlockSpec((None, D_MODEL, D_MODEL), lambda b, i: (l, 0, 0)),
            pl.BlockSpec((None, HEADS, CHUNK, CHUNK), lambda b, i: (l, 0, 0, 0)),
            pl.BlockSpec((None, CHUNK, HEADS), lambda b, i: (l, 0, 0)),
            vec(GW), vec(GW),
            pl.BlockSpec((None, len(POOL_WINDOWS), LANES, LANES), lambda b, i: (l, 0, 0, 0)),
            vec(GW),
            pl.BlockSpec((None, CONV_WIDTH, GW), lambda b, i: (l, 0, 0)),
            vec(GW), vec(GW), vec(GW), vec(D_MODEL), vec(D_MODEL),
        ],
        out_specs=[
            pl.BlockSpec((TS_MIX, D_MODEL), lambda b, i: (b * ns + i, 0)),
            pl.BlockSpec((TS_MIX, D_MODEL), lambda b, i: (b * ns + i, 0)),
            pl.BlockSpec((TS_MIX * SUBLANES, LANES), lambda b, i: (b * ns + i, 0)),
        ],
        out_shape=[
            jax.ShapeDtypeStruct((n, D_MODEL), F32),
            jax.ShapeDtypeStruct((n, D_MODEL), BF16),
            jax.ShapeDtypeStruct((n * SUBLANES, LANES), U32),
        ],
        scratch_shapes=[
            pltpu.VMEM((TS_MIX, D_MODEL), BF16),
            pltpu.VMEM((TS_MIX + HALO, GW), F32),
        ],
        compiler_params=pltpu.CompilerParams(vmem_limit_bytes=VMEM_LIMIT),
        name="mixers_outproj_ln1",
    )(h6, h6, kv, x, wout_bf, p['sgu_w'], p['sgu_b'], p['sgu_ln_g'], p['sgu_ln_b'],
      p['pool_w'], p['pool_scale'], p['conv_w'], p['conv_b'], p['conv_ln_g'], p['conv_ln_b'],
      p['ln1_g'], p['ln1_b'])


def _split_bf16(a):
    hi = a.astype(BF16)
    lo = (a - hi.astype(F32)).astype(BF16)
    return hi, lo


def _router_kernel(x_ref, wrt_ref, bias_ref, idx_ref, w_ref, pos_ref, cnt_ref, carry_ref):
    t = x_ref.shape[0]
    neg = -jnp.inf

    @pl.when(pl.program_id(0) == 0)
    def _():
        carry_ref[...] = jnp.zeros_like(carry_ref)

    xh, xl = _split_bf16(x_ref[...])
    wh, wl = _split_bf16(wrt_ref[...])
    logits = _dot_nt(wh, xh) + (_dot_nt(wh, xl) + _dot_nt(wl, xh))
    scores = jax.nn.sigmoid(logits)
    choice = scores + bias_ref[...]

    iota8 = lax.broadcasted_iota(I32, (GROUP_SIZE, t), 0)
    rows = []
    for g in range(N_GROUPS):
        cg = choice[g * GROUP_SIZE:(g + 1) * GROUP_SIZE]
        m1 = jnp.max(cg, axis=0, keepdims=True)
        i1 = jnp.min(jnp.where(cg == m1, iota8, GROUP_SIZE), axis=0, keepdims=True)
        m2 = jnp.max(jnp.where(iota8 == i1, neg, cg), axis=0, keepdims=True)
        rows.append(m1 + m2)
    gwork = jnp.concatenate(rows, axis=0)

    gsel = jnp.zeros((N_GROUPS, t), jnp.bool_)
    for _ in range(TOPK_GROUPS):
        m = jnp.max(gwork, axis=0, keepdims=True)
        ii = jnp.min(jnp.where(gwork == m, iota8, N_GROUPS), axis=0, keepdims=True)
        hit = iota8 == ii
        gsel = gsel | hit
        gwork = jnp.where(hit, neg, gwork)
    emask = jnp.concatenate(
        [jnp.broadcast_to(gsel[g:g + 1], (GROUP_SIZE, t)) for g in range(N_GROUPS)], axis=0)

    iota_e = lax.broadcasted_iota(I32, (N_EXPERTS, t), 0)
    work = jnp.where(emask, choice, neg)
    sel = jnp.zeros((N_EXPERTS, t), jnp.bool_)
    idx_rows, raw_rows = [], []
    for _ in range(TOP_K):
        m = jnp.max(work, axis=0, keepdims=True)
        ii = jnp.min(jnp.where(work == m, iota_e, N_EXPERTS), axis=0, keepdims=True)
        hit = iota_e == ii
        idx_rows.append(ii)
        raw_rows.append(jnp.sum(jnp.where(hit, scores, 0.0), axis=0, keepdims=True))
        sel = sel | hit
        work = jnp.where(hit, neg, work)
    idx = jnp.concatenate(idx_rows, axis=0)
    raw = jnp.concatenate(raw_rows, axis=0)
    wsum = raw_rows[0]
    for r in raw_rows[1:]:
        wsum = wsum + r
    idx_ref[...] = idx
    w_ref[...] = raw / (wsum + 1e-20) * ROUTED_SCALE

    upper = (lax.broadcasted_iota(I32, (t, t), 0) < lax.broadcasted_iota(I32, (t, t), 1))
    sel_f = sel.astype(F32)
    prefix = _dot(sel_f.astype(BF16), upper.astype(F32).astype(BF16)) + carry_ref[...]
    pos_rows = [jnp.sum(jnp.where(iota_e == idx_rows[k], prefix, 0.0), axis=0, keepdims=True)
                for k in range(TOP_K)]
    pos_ref[...] = jnp.concatenate(pos_rows, axis=0).astype(I32)
    carry = carry_ref[...] + jnp.sum(sel_f, axis=1, keepdims=True)
    carry_ref[...] = carry
    cnt_ref[...] = jnp.broadcast_to(carry, cnt_ref.shape).astype(I32)


def _router(x1, w_router_t, bias, l):
    n = x1.shape[0]
    tok_spec = pl.BlockSpec((TOP_K, T_ROUTE), lambda t: (0, t))
    return pl.pallas_call(
        _router_kernel,
        grid=(n // T_ROUTE,),
        in_specs=[
            pl.BlockSpec((T_ROUTE, D_MODEL), lambda t: (t, 0)),
            pl.BlockSpec((None, N_EXPERTS, D_MODEL), lambda t: (l, 0, 0)),
            pl.BlockSpec((None, N_EXPERTS, 1), lambda t: (l, 0, 0)),
        ],
        out_specs=[tok_spec, tok_spec, tok_spec,
                   pl.BlockSpec((N_EXPERTS, LANES), lambda t: (0, 0))],
        out_shape=[
            jax.ShapeDtypeStruct((TOP_K, n), I32),
            jax.ShapeDtypeStruct((TOP_K, n), F32),
            jax.ShapeDtypeStruct((TOP_K, n), I32),
            jax.ShapeDtypeStruct((N_EXPERTS, LANES), I32),
        ],
        scratch_shapes=[pltpu.VMEM((N_EXPERTS, 1), F32)],
        compiler_params=pltpu.CompilerParams(vmem_limit_bytes=VMEM_LIMIT),
        name="router",
    )(x1, w_router_t, bias)


def _dispatch(idx_t, w_t, pos_t, cnt, n):
    nb = n * TOP_K // BM_EXP + N_EXPERTS
    counts = cnt[:, 0]
    padded = (counts + BM_EXP - 1) // BM_EXP * BM_EXP
    pad_end = jnp.cumsum(padded)
    pad_start = pad_end - padded
    dest = (pad_start[idx_t] + pos_t).reshape(-1)
    tok = jnp.broadcast_to(jnp.arange(n, dtype=I32)[None, :], (TOP_K, n)).reshape(-1)
    slot_tok = jnp.full((nb * BM_EXP,), n, I32).at[dest].set(tok)
    slot_w = jnp.zeros((nb * BM_EXP,), F32).at[dest].set(w_t.reshape(-1))
    blk_start = jnp.arange(nb, dtype=I32) * BM_EXP
    blk_e = jnp.minimum(jnp.searchsorted(pad_end, blk_start, side='right'), N_EXPERTS - 1).astype(I32)
    nvalid = jnp.clip(pad_start[blk_e] + counts[blk_e] - blk_start, 0, BM_EXP).astype(I32)
    return (slot_tok.reshape(nb, 1, BM_EXP), slot_w.reshape(nb, 1, BM_EXP), blk_e, nvalid)


def _e1_kernel(blk_e_ref, nvalid_ref, tok_ref, xpk_ref, wg_ref, wu_ref, h_ref, stage_ref, lhs_ref, wcat_ref):
    b = pl.program_id(0)
    nv = nvalid_ref[b]
    e = blk_e_ref[b]
    e_prev = blk_e_ref[jnp.maximum(b - 1, 0)]
    n_tok = xpk_ref.shape[0] // SUBLANES

    @pl.when(jnp.logical_or(b == 0, e != e_prev))
    def _():
        wcat_ref[:, :D_EXPERT] = wg_ref[...].astype(BF16)
        wcat_ref[:, D_EXPERT:] = wu_ref[...].astype(BF16)

    @pl.when(nv > 0)
    def _():
        def gather(r, carry):
            t = jnp.minimum(tok_ref[0, 0, r], n_tok - 1)
            src = pl.ds(pl.multiple_of(t * SUBLANES, SUBLANES), SUBLANES)
            dst = pl.ds(pl.multiple_of(r * SUBLANES, SUBLANES), SUBLANES)
            stage_ref[dst, :] = xpk_ref[src, :]
            return carry

        lax.fori_loop(0, BM_EXP, gather, 0, unroll=8)
        for s in range(SUBLANES):
            word = stage_ref[pl.ds(s, BM_EXP, stride=SUBLANES), :]
            lo = lax.bitcast_convert_type(word << 16, F32)
            hi = lax.bitcast_convert_type(word & jnp.uint32(HI_MASK), F32)
            lhs_ref[:, s * LANES:(s + 1) * LANES] = lo.astype(BF16)
            lhs_ref[:, HALF + s * LANES:HALF + (s + 1) * LANES] = hi.astype(BF16)
        gu = _dot(lhs_ref[...], wcat_ref[...])
        h_ref[...] = (jax.nn.silu(gu[:, :D_EXPERT]) * gu[:, D_EXPERT:]).astype(BF16)

    @pl.when(nv == 0)
    def _():
        h_ref[...] = jnp.zeros_like(h_ref)


def _expert_gate_up(blk_e, nvalid, slot_tok, xpk, exp_w_gate, exp_w_up, l):
    nb = blk_e.shape[0]
    grid_spec = pltpu.PrefetchScalarGridSpec(
        num_scalar_prefetch=2,
        grid=(nb,),
        in_specs=[
            pl.BlockSpec((1, 1, BM_EXP), lambda b, be, nv: (b, 0, 0), memory_space=pltpu.SMEM),
            pl.BlockSpec(xpk.shape, lambda b, be, nv: (0, 0), pipeline_mode=pl.Buffered(1)),
            pl.BlockSpec((None, None, D_MODEL, D_EXPERT), lambda b, be, nv: (l, be[b], 0, 0)),
            pl.BlockSpec((None, None, D_MODEL, D_EXPERT), lambda b, be, nv: (l, be[b], 0, 0)),
        ],
        out_specs=pl.BlockSpec((BM_EXP, D_EXPERT), lambda b, be, nv: (b, 0)),
        scratch_shapes=[
            pltpu.VMEM((BM_EXP * SUBLANES, LANES), U32),
            pltpu.VMEM((BM_EXP, D_MODEL), BF16),
            pltpu.VMEM((D_MODEL, 2 * D_EXPERT), BF16),
        ],
    )
    return pl.pallas_call(
        _e1_kernel,
        grid_spec=grid_spec,
        out_shape=jax.ShapeDtypeStruct((nb * BM_EXP, D_EXPERT), BF16),
        compiler_params=pltpu.CompilerParams(vmem_limit_bytes=VMEM_LIMIT),
        name="expert_gate_up",
    )(blk_e, nvalid, slot_tok, xpk, exp_w_gate, exp_w_up)


def _e2_kernel(blk_e_ref, nvalid_ref, tok_ref, w_ref, h_ref, wd_ref, out_ref, stage_ref):
    b = pl.program_id(1)
    nv = nvalid_ref[b]
    group = SUBLANES

    @pl.when(b == 0)
    def _():
        out_ref[...] = jnp.zeros_like(out_ref)

    @pl.when(nv > 0)
    def _():
        y = _dot(h_ref[...], wd_ref[...].astype(BF16))
        for s in range(SUBLANES):
            stage_ref[pl.ds(s, BM_EXP, stride=SUBLANES), :] = y[:, s * LANES:(s + 1) * LANES]

        def scatter(gi, carry):
            dsts, vals = [], []
            for k in range(group):
                r = gi * group + k
                dst = pl.ds(pl.multiple_of(tok_ref[0, 0, r] * SUBLANES, SUBLANES), SUBLANES)
                src = pl.ds(pl.multiple_of(r * SUBLANES, SUBLANES), SUBLANES)
                dsts.append(dst)
                vals.append(out_ref[dst, :] + w_ref[0, 0, r] * stage_ref[src, :])
            for dst, val in zip(dsts, vals):
                out_ref[dst, :] = val
            return carry

        lax.fori_loop(0, (nv + group - 1) // group, scatter, 0)


def _expert_down_combine(blk_e, nvalid, slot_tok, slot_w, hmid, exp_w_down, l, n):
    nb = blk_e.shape[0]
    rows = (n + 1) * SUBLANES
    grid_spec = pltpu.PrefetchScalarGridSpec(
        num_scalar_prefetch=2,
        grid=(D_MODEL // HALF, nb),
        in_specs=[
            pl.BlockSpec((1, 1, BM_EXP), lambda c, b, be, nv: (b, 0, 0), memory_space=pltpu.SMEM),
            pl.BlockSpec((1, 1, BM_EXP), lambda c, b, be, nv: (b, 0, 0), memory_space=pltpu.SMEM),
            pl.BlockSpec((BM_EXP, D_EXPERT), lambda c, b, be, nv: (b, 0)),
            pl.BlockSpec((None, None, D_EXPERT, HALF), lambda c, b, be, nv: (l, be[b], 0, c)),
        ],
        out_specs=pl.BlockSpec((None, rows, LANES), lambda c, b, be, nv: (c, 0, 0),
                               pipeline_mode=pl.Buffered(1)),
        scratch_shapes=[pltpu.VMEM((BM_EXP * SUBLANES, LANES), F32)],
    )
    return pl.pallas_call(
        _e2_kernel,
        grid_spec=grid_spec,
        out_shape=jax.ShapeDtypeStruct((D_MODEL // HALF, rows, LANES), F32),
        compiler_params=pltpu.CompilerParams(vmem_limit_bytes=VMEM_LIMIT),
        name="expert_down_combine",
    )(blk_e, nvalid, slot_tok, slot_w, hmid, exp_w_down)


def _final_kernel(x1_ref, x1bf_ref, routed_ref, wgu_ref, wd_ref, g_ref, b_ref, x2_ref, x2bf_ref):
    ts = x1_ref.shape[0]
    gu = _dot(x1bf_ref[...], wgu_ref[...])
    hs = (jax.nn.silu(gu[:, :D_SHARED]) * gu[:, D_SHARED:]).astype(BF16)
    shared = _dot(hs, wd_ref[...])
    routed = jnp.concatenate(
        [routed_ref[c, pl.ds(s, ts, stride=SUBLANES), :]
         for c in range(D_MODEL // HALF) for s in range(SUBLANES)], axis=1)
    z = ALPHA * x1_ref[...] + (shared + routed)
    x2 = _ln(z, g_ref[...], b_ref[...])
    x2_ref[...] = x2
    x2bf_ref[...] = x2.astype(BF16)


def _shared_ln2(x1, x1bf, routed, sh_gu_bf, sh_d_bf, ln2_g, ln2_b, l):
    n = x1.shape[0]
    row_spec = pl.BlockSpec((TS_FIN, D_MODEL), lambda i: (i, 0))
    return pl.pallas_call(
        _final_kernel,
        grid=(n // TS_FIN,),
        in_specs=[
            row_spec, row_spec,
            pl.BlockSpec((D_MODEL // HALF, TS_FIN * SUBLANES, LANES), lambda i: (0, i, 0)),
            pl.BlockSpec((None, D_MODEL, 2 * D_SHARED), lambda i: (l, 0, 0)),
            pl.BlockSpec((None, D_SHARED, D_MODEL), lambda i: (l, 0, 0)),
            pl.BlockSpec((None, 1, D_MODEL), lambda i: (l, 0, 0)),
            pl.BlockSpec((None, 1, D_MODEL), lambda i: (l, 0, 0)),
        ],
        out_specs=[row_spec, row_spec],
        out_shape=[jax.ShapeDtypeStruct((n, D_MODEL), F32), jax.ShapeDtypeStruct((n, D_MODEL), BF16)],
        compiler_params=pltpu.CompilerParams(vmem_limit_bytes=VMEM_LIMIT),
        name="shared_ln2",
    )(x1, x1bf, routed, sh_gu_bf, sh_d_bf, ln2_g, ln2_b)


def kernel(x, mem, mem_ln_g, mem_ln_b, w_in, sgu_ln_g, sgu_ln_b, sgu_w, sgu_b, pool_w, pool_scale, conv_w, conv_b, conv_ln_g, conv_ln_b, w_mem_kv, w_out, ln1_g, ln1_b, w_router, router_bias, exp_w_gate, exp_w_up, exp_w_down, sh_w_gate, sh_w_up, sh_w_down, ln2_g, ln2_b):
    batch, seq, d = x.shape
    n = batch * seq
    depth = w_in.shape[0]
    assert (d, depth) == (D_MODEL, DEPTH) and seq % TS_MIX == 0 and n % BM_IN == 0 and n % T_ROUTE == 0

    w_in_p = jnp.concatenate([w_in[..., :2 * GW], w_in[..., 5 * GW:], w_in[..., 2 * GW:5 * GW]], axis=-1).astype(BF16)
    wout_bf = w_out.astype(BF16)
    sh_gu_bf = jnp.concatenate([sh_w_gate, sh_w_up], axis=-1).astype(BF16)
    sh_d_bf = sh_w_down.astype(BF16)
    w_router_t = jnp.swapaxes(w_router, 1, 2)
    bias3 = router_bias[..., None]

    def row(a):
        return a[:, None, :]

    p = dict(sgu_w=sgu_w, sgu_b=sgu_b, sgu_ln_g=row(sgu_ln_g), sgu_ln_b=row(sgu_ln_b), pool_w=pool_w,
             pool_scale=row(pool_scale), conv_w=conv_w, conv_b=row(conv_b), conv_ln_g=row(conv_ln_g),
             conv_ln_b=row(conv_ln_b), ln1_g=row(ln1_g), ln1_b=row(ln1_b))
    ln2_g3, ln2_b3 = row(ln2_g), row(ln2_b)

    kv = _memory_kv(mem.reshape(batch * N_MEM, d), mem_ln_g[None, :], mem_ln_b[None, :], w_mem_kv)

    xf = x.reshape(n, d)
    xbf = xf.astype(BF16)
    for l in range(depth):
        h6 = _in_proj(xbf, w_in_p, l)
        x1, x1bf, xpk = _mixers(h6, kv, xf, wout_bf, p, l, batch, seq)
        idx_t, w_t, pos_t, cnt = _router(x1, w_router_t, bias3, l)
        slot_tok, slot_w, blk_e, nvalid = _dispatch(idx_t, w_t, pos_t, cnt, n)
        hmid = _expert_gate_up(blk_e, nvalid, slot_tok, xpk, exp_w_gate, exp_w_up, l)
        routed = _expert_down_combine(blk_e, nvalid, slot_tok, slot_w, hmid, exp_w_down, l, n)
        xf, xbf = _shared_ln2(x1, x1bf, routed, sh_gu_bf, sh_d_bf, ln2_g3, ln2_b3, l)
    return xf.reshape(batch, seq, d)
```

```python
import functools

import jax
import jax.numpy as jnp
from jax import lax
from jax.experimental import pallas as pl
from jax.experimental.pallas import tpu as pltpu

D_MODEL = 2048
DEPTH = 4
GW = 512
HEAD_DIM = 128
HEADS = 4
CHUNK = 128
POOL_WINDOWS = (2, 4, 8, 16)
CONV_WIDTH = 31
N_MEM = 256
N_EXPERTS = 64
N_GROUPS = 8
GROUP_SIZE = N_EXPERTS // N_GROUPS
TOPK_GROUPS = 4
TOP_K = 8
D_EXPERT = 256
D_SHARED = 256
ROUTED_SCALE = 2.5
LN_EPS = 1e-5
ALPHA = (2.0 * DEPTH) ** 0.25

LANES = 128
SUBLANES = 8
HALF = D_MODEL // 2
HALO = 32
VMEM_LIMIT = 56 * 1024 * 1024

TS_MIX = 256
BM_IN = 512
BN_IN = 1024
T_ROUTE = 512
BM_EXP = 256
BM_SHIFT = 8
TS_FIN = 256
SLOT_UNROLL = 16

F32 = jnp.float32
BF16 = jnp.bfloat16
U32 = jnp.uint32
I32 = jnp.int32


def _ln(x, g, b):
    mu = jnp.mean(x, axis=-1, keepdims=True)
    xc = x - mu
    var = jnp.mean(xc * xc, axis=-1, keepdims=True)
    return xc * lax.rsqrt(var + LN_EPS) * g + b


def _dot(a, b):
    return jnp.dot(a, b, preferred_element_type=F32)


def _dot_nt(a, b):
    return lax.dot_general(a, b, (((1,), (1,)), ((), ())), preferred_element_type=F32)


def _kv_kernel(mem_ref, g_ref, b_ref, w_ref, o_ref):
    memn = _ln(mem_ref[...], g_ref[...], b_ref[...])
    o_ref[...] = _dot(memn.astype(BF16), w_ref[...].astype(BF16)).astype(BF16)


def _memory_kv(mem2, g, b, w_mem_kv):
    nb = mem2.shape[0] // N_MEM
    return pl.pallas_call(
        _kv_kernel,
        grid=(DEPTH, nb),
        in_specs=[
            pl.BlockSpec((N_MEM, D_MODEL), lambda l, i: (i, 0)),
            pl.BlockSpec((1, D_MODEL), lambda l, i: (0, 0)),
            pl.BlockSpec((1, D_MODEL), lambda l, i: (0, 0)),
            pl.BlockSpec((None, D_MODEL, 2 * GW), lambda l, i: (l, 0, 0)),
        ],
        out_specs=pl.BlockSpec((None, N_MEM, 2 * GW), lambda l, i: (l, i, 0)),
        out_shape=jax.ShapeDtypeStruct((DEPTH, mem2.shape[0], 2 * GW), BF16),
        compiler_params=pltpu.CompilerParams(vmem_limit_bytes=VMEM_LIMIT),
        name="memory_kv",
    )(mem2, g, b, w_mem_kv)


def _inproj_kernel(x_ref, w_ref, o_ref):
    acc = _dot(x_ref[...], w_ref[...])
    o_ref[0] = acc[:, :GW]
    o_ref[1] = acc[:, GW:]


def _in_proj(xbf, w_in_p, l):
    n = xbf.shape[0]
    return pl.pallas_call(
        _inproj_kernel,
        grid=(6 * GW // BN_IN, n // BM_IN),
        in_specs=[
            pl.BlockSpec((BM_IN, D_MODEL), lambda j, i: (i, 0)),
            pl.BlockSpec((None, D_MODEL, BN_IN), lambda j, i: (l, 0, j)),
        ],
        out_specs=pl.BlockSpec((BN_IN // GW, BM_IN, GW), lambda j, i: (j, i, 0)),
        out_shape=jax.ShapeDtypeStruct((6, n, GW), F32),
        compiler_params=pltpu.CompilerParams(vmem_limit_bytes=VMEM_LIMIT),
        name="in_proj",
    )(xbf, w_in_p)


def _pack_tokens(x, xpk_ref):
    rows = x.shape[0]
    for s in range(SUBLANES):
        lo = x[:, s * LANES:(s + 1) * LANES]
        hi = x[:, HALF + s * LANES:HALF + (s + 1) * LANES]
        xpk_ref[pl.ds(s, rows, stride=SUBLANES), :] = pltpu.pack_elementwise([lo, hi], packed_dtype=BF16)


def _unpack_tokens(word):
    lo = pltpu.unpack_elementwise(word, index=0, packed_dtype=BF16, unpacked_dtype=F32)
    hi = pltpu.unpack_elementwise(word, index=1, packed_dtype=BF16, unpacked_dtype=F32)
    return lo.astype(BF16), hi.astype(BF16)


def _mix_kernel(h_ref, halo_ref, kv_ref, x_ref, wout_ref, sguw_ref, sgub_ref, sgug_ref, sgubeta_ref,
                poolw_ref, pools_ref, convw_ref, convb_ref, cvg_ref, cvb_ref, ln1g_ref, ln1b_ref,
                x1_ref, x1bf_ref, xpk_ref, mix_ref, cbuf_ref):
    i = pl.program_id(1)
    ts = h_ref.shape[1]
    first = i == 0

    u = jax.nn.gelu(h_ref[0])
    v = jax.nn.gelu(h_ref[1])
    tri = (lax.broadcasted_iota(I32, (CHUNK, CHUNK), 0) >= lax.broadcasted_iota(I32, (CHUNK, CHUNK), 1))
    for hd in range(HEADS):
        cs = slice(hd * HEAD_DIM, (hd + 1) * HEAD_DIM)
        vn = _ln(v[:, cs], sgug_ref[:, cs], sgubeta_ref[:, cs]).astype(BF16)
        wm = jnp.where(tri, sguw_ref[hd], 0.0).astype(BF16)
        bcol = sgub_ref[:, hd:hd + 1]
        for c in range(ts // CHUNK):
            rs = slice(c * CHUNK, (c + 1) * CHUNK)
            mixed = _dot(wm, vn[rs]) + bcol
            mix_ref[rs, cs] = (u[rs, cs] * mixed).astype(BF16)

    hp = h_ref[3]
    halo_p = jnp.where(first, 0.0, halo_ref[0])
    ext = jnp.concatenate([halo_p, hp], axis=0)
    pos1 = (i * ts + lax.broadcasted_iota(I32, (ts, 1), 0) + 1).astype(F32)
    for g, win in enumerate(POOL_WINDOWS):
        cs = slice(g * LANES, (g + 1) * LANES)
        s = ext[:, cs]
        sh = 1
        while sh < win:
            s = s + pltpu.roll(s, sh, 0)
            sh *= 2
        cnt = jnp.minimum(pos1, float(win))
        pooled = s[HALO:] / cnt - hp[:, cs]
        y = _dot(pooled.astype(BF16), poolw_ref[g].astype(BF16)) * pools_ref[:, cs]
        mix_ref[:, GW + g * LANES:GW + (g + 1) * LANES] = y.astype(BF16)

    glu_ext = jnp.concatenate([jnp.where(first, 0.0, halo_ref[1] * jax.nn.sigmoid(halo_ref[2])),
                               h_ref[4] * jax.nn.sigmoid(h_ref[5])], axis=0)
    cbuf_ref[0] = glu_ext
    for r in range(1, SUBLANES):
        cbuf_ref[r] = pltpu.roll(glu_ext, ts + HALO - r, 0)
    rchunk = 64
    for g in range(GW // LANES):
        cs = slice(g * LANES, (g + 1) * LANES)
        parts = []
        for r0 in range(0, ts, rchunk):
            acc = jnp.zeros((rchunk, LANES), F32)
            for j in range(CONV_WIDTH):
                lead = HALO - (CONV_WIDTH - 1) + j
                start = r0 + lead - lead % SUBLANES
                acc = acc + convw_ref[j:j + 1, cs] * cbuf_ref[lead % SUBLANES, start:start + rchunk, cs]
            parts.append(acc)
        y = jnp.concatenate(parts, axis=0) + convb_ref[:, cs]
        y = _ln(y, cvg_ref[:, cs], cvb_ref[:, cs])
        mix_ref[:, 2 * GW + g * LANES:2 * GW + (g + 1) * LANES] = jax.nn.silu(y).astype(BF16)

    q = h_ref[2]
    for hd in range(HEADS):
        cs = slice(hd * HEAD_DIM, (hd + 1) * HEAD_DIM)
        kh = kv_ref[:, hd * HEAD_DIM:(hd + 1) * HEAD_DIM]
        vh = kv_ref[:, GW + hd * HEAD_DIM:GW + (hd + 1) * HEAD_DIM]
        sc = _dot_nt(q[:, cs].astype(BF16), kh) * (HEAD_DIM ** -0.5)
        e = jnp.exp(sc - jnp.max(sc, axis=-1, keepdims=True))
        pr = e * (1.0 / jnp.sum(e, axis=-1, keepdims=True))
        mix_ref[:, 3 * GW + hd * HEAD_DIM:3 * GW + (hd + 1) * HEAD_DIM] = _dot(pr.astype(BF16), vh).astype(BF16)

    z = ALPHA * x_ref[...] + _dot(mix_ref[...], wout_ref[...])
    x1 = _ln(z, ln1g_ref[...], ln1b_ref[...])
    x1_ref[...] = x1
    x1bf_ref[...] = x1.astype(BF16)
    _pack_tokens(x1, xpk_ref)


def _mixers(h6, kv, x, wout_bf, p, l, batch, seq):
    n = x.shape[0]
    ns = seq // TS_MIX
    hb = TS_MIX // HALO

    def vec(width):
        return pl.BlockSpec((None, 1, width), lambda b, i: (l, 0, 0))

    return pl.pallas_call(
        _mix_kernel,
        grid=(batch, ns),
        in_specs=[
            pl.BlockSpec((6, TS_MIX, GW), lambda b, i: (0, b * ns + i, 0)),
            pl.BlockSpec((3, HALO, GW), lambda b, i: (1, jnp.maximum((b * ns + i) * hb - 1, 0), 0)),
            pl.BlockSpec((None, N_MEM, 2 * GW), lambda b, i: (l, b, 0)),
            pl.BlockSpec((TS_MIX, D_MODEL), lambda b, i: (b * ns + i, 0)),
            pl.BlockSpec((None, D_MODEL, D_MODEL), lambda b, i: (l, 0, 0)),
            pl.BlockSpec((None, HEADS, CHUNK, CHUNK), lambda b, i: (l, 0, 0, 0)),
            pl.BlockSpec((None, CHUNK, HEADS), lambda b, i: (l, 0, 0)),
            vec(GW), vec(GW),
            pl.BlockSpec((None, len(POOL_WINDOWS), LANES, LANES), lambda b, i: (l, 0, 0, 0)),
            vec(GW),
            pl.BlockSpec((None, CONV_WIDTH, GW), lambda b, i: (l, 0, 0)),
            vec(GW), vec(GW), vec(GW), vec(D_MODEL), vec(D_MODEL),
        ],
        out_specs=[
            pl.BlockSpec((TS_MIX, D_MODEL), lambda b, i: (b * ns + i, 0)),
            pl.BlockSpec((TS_MIX, D_MODEL), lambda b, i: (b * ns + i, 0)),
            pl.BlockSpec((TS_MIX * SUBLANES, LANES), lambda b, i: (b * ns + i, 0)),
        ],
        out_shape=[
            jax.ShapeDtypeStruct((n, D_MODEL), F32),
            jax.ShapeDtypeStruct((n, D_MODEL), BF16),
            jax.ShapeDtypeStruct((n * SUBLANES, LANES), U32),
        ],
        scratch_shapes=[
            pltpu.VMEM((TS_MIX, D_MODEL), BF16),
            pltpu.VMEM((SUBLANES, TS_MIX + HALO, GW), F32),
        ],
        compiler_params=pltpu.CompilerParams(vmem_limit_bytes=VMEM_LIMIT),
        name="mixers_outproj_ln1",
    )(h6, h6, kv, x, wout_bf, p['sgu_w'], p['sgu_b'], p['sgu_ln_g'], p['sgu_ln_b'],
      p['pool_w'], p['pool_scale'], p['conv_w'], p['conv_b'], p['conv_ln_g'], p['conv_ln_b'],
      p['ln1_g'], p['ln1_b'])


def _split_bf16(a):
    hi = a.astype(BF16)
    lo = (a - hi.astype(F32)).astype(BF16)
    return hi, lo


def _router_kernel(x_ref, wrt_ref, bias_ref, dest_ref, wdense_ref, emeta_ref, carry_ref, idx_ref):
    t = x_ref.shape[0]
    neg = -jnp.inf
    step = pl.program_id(0)

    @pl.when(step == 0)
    def _():
        carry_ref[...] = jnp.zeros_like(carry_ref)

    xh, xl = _split_bf16(x_ref[...])
    wh, wl = _split_bf16(wrt_ref[...])
    logits = _dot_nt(wh, xh) + (_dot_nt(wh, xl) + _dot_nt(wl, xh))
    scores = jax.nn.sigmoid(logits)
    choice = scores + bias_ref[...]

    iota8 = lax.broadcasted_iota(I32, (GROUP_SIZE, t), 0)
    rows = []
    for g in range(N_GROUPS):
        cg = choice[g * GROUP_SIZE:(g + 1) * GROUP_SIZE]
        m1 = jnp.max(cg, axis=0, keepdims=True)
        i1 = jnp.min(jnp.where(cg == m1, iota8, GROUP_SIZE), axis=0, keepdims=True)
        m2 = jnp.max(jnp.where(iota8 == i1, neg, cg), axis=0, keepdims=True)
        rows.append(m1 + m2)
    gwork = jnp.concatenate(rows, axis=0)

    gsel = jnp.zeros((N_GROUPS, t), jnp.bool_)
    for _ in range(TOPK_GROUPS):
        m = jnp.max(gwork, axis=0, keepdims=True)
        ii = jnp.min(jnp.where(gwork == m, iota8, N_GROUPS), axis=0, keepdims=True)
        hit = iota8 == ii
        gsel = gsel | hit
        gwork = jnp.where(hit, neg, gwork)
    emask = jnp.concatenate(
        [jnp.broadcast_to(gsel[g:g + 1], (GROUP_SIZE, t)) for g in range(N_GROUPS)], axis=0)

    iota_e = lax.broadcasted_iota(I32, (N_EXPERTS, t), 0)
    work = jnp.where(emask, choice, neg)
    sel = jnp.zeros((N_EXPERTS, t), jnp.bool_)
    idx_rows, raw_rows = [], []
    for _ in range(TOP_K):
        m = jnp.max(work, axis=0, keepdims=True)
        ii = jnp.min(jnp.where(work == m, iota_e, N_EXPERTS), axis=0, keepdims=True)
        hit = iota_e == ii
        idx_rows.append(ii)
        raw_rows.append(jnp.sum(jnp.where(hit, scores, 0.0), axis=0, keepdims=True))
        sel = sel | hit
        work = jnp.where(hit, neg, work)
    wsum = raw_rows[0]
    for r in raw_rows[1:]:
        wsum = wsum + r
    scale = ROUTED_SCALE / (wsum + 1e-20)

    dense = jnp.zeros((N_EXPERTS, t), F32)
    for k in range(TOP_K):
        dense = jnp.where(iota_e == idx_rows[k], raw_rows[k] * scale, dense)
    wdense_ref[...] = jnp.concatenate([dense, jnp.zeros((LANES - N_EXPERTS, t), F32)], axis=0).T

    upper = (lax.broadcasted_iota(I32, (t, t), 0) < lax.broadcasted_iota(I32, (t, t), 1))
    sel_f = sel.astype(F32)
    prefix = _dot(sel_f.astype(BF16), upper.astype(F32).astype(BF16)) + carry_ref[...]
    pos_rows = [jnp.sum(jnp.where(iota_e == idx_rows[k], prefix, 0.0), axis=0, keepdims=True)
                for k in range(TOP_K)]
    dest_ref[step] = jnp.concatenate(pos_rows, axis=0).astype(I32)
    idx_ref[step] = jnp.concatenate(idx_rows, axis=0)
    carry = carry_ref[...] + jnp.sum(sel_f, axis=1, keepdims=True)
    carry_ref[...] = carry

    @pl.when(step == pl.num_programs(0) - 1)
    def _():
        cnt = carry.astype(I32)
        nblk = (cnt + (BM_EXP - 1)) >> BM_SHIFT
        lower = (lax.broadcasted_iota(I32, (N_EXPERTS, N_EXPERTS), 0)
                 >= lax.broadcasted_iota(I32, (N_EXPERTS, N_EXPERTS), 1))
        nblk_wide = jnp.broadcast_to(nblk.astype(F32), (N_EXPERTS, LANES)).astype(BF16)
        bend = _dot(lower.astype(F32).astype(BF16), nblk_wide)[:, 0:1].astype(I32)
        bstart = bend - nblk
        pad_start = bstart << BM_SHIFT
        pad_rows = jnp.broadcast_to(pad_start, (N_EXPERTS, t))

        def add_start(tile, c):
            idx_t = idx_ref[tile]
            add = jnp.zeros((TOP_K, t), I32)
            for ex in range(N_EXPERTS):
                add = jnp.where(idx_t == ex, pad_rows[ex:ex + 1], add)
            dest_ref[tile] = dest_ref[tile] + add
            return c

        lax.fori_loop(0, dest_ref.shape[0], add_start, 0)

        eye = (lax.broadcasted_iota(I32, (N_EXPERTS, LANES), 0) == lax.broadcasted_iota(I32, (N_EXPERTS, LANES), 1))

        def as_row(col):
            return jnp.sum(jnp.where(eye, col, 0), axis=0, keepdims=True)

        emeta_ref[...] = jnp.concatenate(
            [as_row(pad_start + cnt), as_row(bend << BM_SHIFT), as_row(bstart), as_row(nblk),
             jnp.zeros((SUBLANES - 4, LANES), I32)], axis=0)


def _router(x1, w_router_t, bias, l):
    n = x1.shape[0]
    nt = n // T_ROUTE
    return pl.pallas_call(
        _router_kernel,
        grid=(nt,),
        in_specs=[
            pl.BlockSpec((T_ROUTE, D_MODEL), lambda t: (t, 0)),
            pl.BlockSpec((None, N_EXPERTS, D_MODEL), lambda t: (l, 0, 0)),
            pl.BlockSpec((None, N_EXPERTS, 1), lambda t: (l, 0, 0)),
        ],
        out_specs=[
            pl.BlockSpec((nt, TOP_K, T_ROUTE), lambda t: (0, 0, 0)),
            pl.BlockSpec((T_ROUTE, LANES), lambda t: (t, 0)),
            pl.BlockSpec((SUBLANES, LANES), lambda t: (0, 0)),
        ],
        out_shape=[
            jax.ShapeDtypeStruct((nt, TOP_K, T_ROUTE), I32),
            jax.ShapeDtypeStruct((n, LANES), F32),
            jax.ShapeDtypeStruct((SUBLANES, LANES), I32),
        ],
        scratch_shapes=[pltpu.VMEM((N_EXPERTS, 1), F32), pltpu.VMEM((nt, TOP_K, T_ROUTE), I32)],
        compiler_params=pltpu.CompilerParams(vmem_limit_bytes=VMEM_LIMIT),
        name="router",
    )(x1, w_router_t, bias)


def _slot_kernel(dest_ref, emeta_ref, off_ref, *, n_tok):
    spare = n_tok * SUBLANES
    group = SUBLANES

    def fill_range(start, end):
        def fill(j, c):
            for u in range(group):
                off_ref[start + j * group + u] = spare
            return c
        lax.fori_loop(0, (end - start + group - 1) // group, fill, 0)

    def fill_expert(ex, c):
        fill_range(emeta_ref[0, ex], emeta_ref[1, ex])
        return c

    lax.fori_loop(0, N_EXPERTS, fill_expert, 0)
    fill_range(emeta_ref[1, N_EXPERTS - 1], off_ref.shape[0])

    def tile_body(tile, c):
        for k in range(TOP_K):
            base = (tile * TOP_K + k) * T_ROUTE

            def inner(j, c2, base=base):
                src = base + j * SLOT_UNROLL
                val = (tile * T_ROUTE + j * SLOT_UNROLL) * SUBLANES
                for u in range(SLOT_UNROLL):
                    off_ref[dest_ref[src + u]] = val + u * SUBLANES
                return c2

            lax.fori_loop(0, T_ROUTE // SLOT_UNROLL, inner, 0)
        return c

    lax.fori_loop(0, n_tok // T_ROUTE, tile_body, 0)


def _slot_table(dest_flat, emeta, n, nb):
    smem = pl.BlockSpec(memory_space=pltpu.SMEM)
    return pl.pallas_call(
        functools.partial(_slot_kernel, n_tok=n),
        in_specs=[smem, smem],
        out_specs=smem,
        out_shape=jax.ShapeDtypeStruct(((nb + 1) * BM_EXP,), I32),
        name="slot_table",
    )(dest_flat, emeta)


def _block_rows(first_block, j):
    return pl.ds(pl.multiple_of((first_block + j) * BM_EXP, BM_EXP), BM_EXP)


def _e1_kernel(bstart_ref, nblk_ref, off_ref, xpk_ref, wdense_ref, wg_ref, wu_ref, h_hbm,
               stage_ref, wrow_ref, lhs_ref, wcat_ref, hbuf_ref, sem):
    e = pl.program_id(0)
    nblk = nblk_ref[e]
    first = bstart_ref[e]
    row_mask = xpk_ref.shape[0] - 1

    def h_copy(j, slot):
        return pltpu.make_async_copy(hbuf_ref.at[slot], h_hbm.at[_block_rows(first, j)], sem.at[slot])

    @pl.when(nblk > 0)
    def _():
        wcat_ref[:, :D_EXPERT] = wg_ref[...].astype(BF16)
        wcat_ref[:, D_EXPERT:] = wu_ref[...].astype(BF16)
        lane = lax.broadcasted_iota(I32, (BM_EXP, LANES), 1)

        def block(j, carry):
            slot = j & 1
            base = (first + j) * BM_EXP

            @pl.when(j >= 2)
            def _():
                h_copy(j - 2, slot).wait()

            for r in range(BM_EXP):
                off = off_ref[base + r] & row_mask
                stage_ref[r * SUBLANES:(r + 1) * SUBLANES, :] = xpk_ref[pl.ds(pl.multiple_of(off, SUBLANES), SUBLANES), :]
                wrow_ref[r:r + 1, :] = wdense_ref[pl.ds(off >> 3, 1), :]
            wcol = jnp.sum(jnp.where(lane == e, wrow_ref[...], 0.0), axis=1, keepdims=True)
            for s in range(SUBLANES):
                lo, hi = _unpack_tokens(stage_ref[pl.ds(s, BM_EXP, stride=SUBLANES), :])
                lhs_ref[:, s * LANES:(s + 1) * LANES] = lo
                lhs_ref[:, HALF + s * LANES:HALF + (s + 1) * LANES] = hi
            gu = _dot(lhs_ref[...], wcat_ref[...])
            hbuf_ref[slot] = (jax.nn.silu(gu[:, :D_EXPERT]) * gu[:, D_EXPERT:] * wcol).astype(BF16)
            h_copy(j, slot).start()
            return carry

        lax.fori_loop(0, nblk, block, 0)

        @pl.when(nblk >= 2)
        def _():
            h_copy(nblk - 2, nblk & 1).wait()

        h_copy(nblk - 1, (nblk - 1) & 1).wait()

    @pl.when(e == N_EXPERTS - 1)
    def _():
        hbuf_ref[0] = jnp.zeros((BM_EXP, D_EXPERT), BF16)

        def zero_block(j, carry):
            copy = pltpu.make_async_copy(hbuf_ref.at[0], h_hbm.at[_block_rows(0, j)], sem.at[0])
            copy.start()
            copy.wait()
            return carry

        lax.fori_loop(first + nblk, h_hbm.shape[0] // BM_EXP, zero_block, 0)


def _expert_gate_up(bstart, nblk, slot_off, xpk, wdense, exp_w_gate, exp_w_up, l):
    assert xpk.shape[0] & (xpk.shape[0] - 1) == 0
    grid_spec = pltpu.PrefetchScalarGridSpec(
        num_scalar_prefetch=3,
        grid=(N_EXPERTS,),
        in_specs=[
            pl.BlockSpec(xpk.shape, lambda e, *_: (0, 0), pipeline_mode=pl.Buffered(1)),
            pl.BlockSpec(wdense.shape, lambda e, *_: (0, 0), pipeline_mode=pl.Buffered(1)),
            pl.BlockSpec((None, None, D_MODEL, D_EXPERT), lambda e, *_: (l, e, 0, 0)),
            pl.BlockSpec((None, None, D_MODEL, D_EXPERT), lambda e, *_: (l, e, 0, 0)),
        ],
        out_specs=pl.BlockSpec(memory_space=pl.ANY),
        scratch_shapes=[
            pltpu.VMEM((BM_EXP * SUBLANES, LANES), U32),
            pltpu.VMEM((BM_EXP, LANES), F32),
            pltpu.VMEM((BM_EXP, D_MODEL), BF16),
            pltpu.VMEM((D_MODEL, 2 * D_EXPERT), BF16),
            pltpu.VMEM((2, BM_EXP, D_EXPERT), BF16),
            pltpu.SemaphoreType.DMA((2,)),
        ],
    )
    return pl.pallas_call(
        _e1_kernel,
        grid_spec=grid_spec,
        out_shape=jax.ShapeDtypeStruct((slot_off.shape[0], D_EXPERT), BF16),
        compiler_params=pltpu.CompilerParams(vmem_limit_bytes=VMEM_LIMIT),
        name="expert_gate_up",
    )(bstart, nblk, slot_off, xpk, wdense, exp_w_gate, exp_w_up)


def _e2_kernel(bstart_ref, nblk_ref, off_ref, h_hbm, wd_ref, out_ref, ybuf_ref, hbuf_ref, wdb_ref, sem):
    e = pl.program_id(1)
    nblk = nblk_ref[e]
    first = bstart_ref[e]
    group = SUBLANES

    def h_copy(j, slot):
        return pltpu.make_async_copy(h_hbm.at[_block_rows(first, j)], hbuf_ref.at[slot], sem.at[slot])

    @pl.when(e == 0)
    def _():
        out_ref[...] = jnp.zeros_like(out_ref)

    @pl.when(nblk > 0)
    def _():
        h_copy(0, 0).start()
        wdb_ref[...] = wd_ref[...].astype(BF16)

        def block(j, carry):
            slot = j & 1
            base = (first + j) * BM_EXP
            h_copy(j, slot).wait()

            @pl.when(j + 1 < nblk)
            def _():
                h_copy(j + 1, 1 - slot).start()

            y = _dot(hbuf_ref[slot], wdb_ref[...])
            for g in range(BM_EXP // SUBLANES):
                for c in range(HALF // LANES):
                    t0 = (g * (HALF // LANES) + c) * SUBLANES
                    ybuf_ref[t0:t0 + SUBLANES, :] = y[g * SUBLANES:(g + 1) * SUBLANES, c * LANES:(c + 1) * LANES]

            for g0 in range(0, BM_EXP, group):
                dsts, vals = [], []
                for r in range(g0, g0 + group):
                    dst = pl.ds(pl.multiple_of(off_ref[base + r], SUBLANES), SUBLANES)
                    row = ybuf_ref[pl.ds((r // SUBLANES) * (HALF // LANES) * SUBLANES + r % SUBLANES,
                                         HALF // LANES, stride=SUBLANES), :]
                    dsts.append(dst)
                    vals.append(out_ref[dst, :] + row)
                for dst, val in zip(dsts, vals):
                    out_ref[dst, :] = val
            return carry

        lax.fori_loop(0, nblk, block, 0)


def _expert_down_combine(bstart, nblk, slot_off, hmid, exp_w_down, l, n):
    rows = (n + 1) * SUBLANES
    grid_spec = pltpu.PrefetchScalarGridSpec(
        num_scalar_prefetch=3,
        grid=(D_MODEL // HALF, N_EXPERTS),
        in_specs=[
            pl.BlockSpec(memory_space=pl.ANY),
            pl.BlockSpec((None, None, D_EXPERT, HALF), lambda c, e, *_: (l, e, 0, c)),
        ],
        out_specs=pl.BlockSpec((None, rows, LANES), lambda c, e, *_: (c, 0, 0),
                               pipeline_mode=pl.Buffered(1)),
        scratch_shapes=[
            pltpu.VMEM((BM_EXP * HALF // LANES, LANES), F32),
            pltpu.VMEM((2, BM_EXP, D_EXPERT), BF16),
            pltpu.VMEM((D_EXPERT, HALF), BF16),
            pltpu.SemaphoreType.DMA((2,)),
        ],
    )
    return pl.pallas_call(
        _e2_kernel,
        grid_spec=grid_spec,
        out_shape=jax.ShapeDtypeStruct((D_MODEL // HALF, rows, LANES), F32),
        compiler_params=pltpu.CompilerParams(vmem_limit_bytes=VMEM_LIMIT),
        name="expert_down_combine",
    )(bstart, nblk, slot_off, hmid, exp_w_down)


def _final_kernel(x1_ref, x1bf_ref, routed_ref, wgu_ref, wd_ref, g_ref, b_ref, x2_ref, x2bf_ref):
    ts = x1_ref.shape[0]
    gu = _dot(x1bf_ref[...], wgu_ref[...])
    hs = (jax.nn.silu(gu[:, :D_SHARED]) * gu[:, D_SHARED:]).astype(BF16)
    shared = _dot(hs, wd_ref[...])
    routed = jnp.concatenate(
        [routed_ref[c, pl.ds(s, ts, stride=SUBLANES), :]
         for c in range(D_MODEL // HALF) for s in range(SUBLANES)], axis=1)
    z = ALPHA * x1_ref[...] + (shared + routed)
    x2 = _ln(z, g_ref[...], b_ref[...])
    x2_ref[...] = x2
    x2bf_ref[...] = x2.astype(BF16)


def _shared_ln2(x1, x1bf, routed, sh_gu_bf, sh_d_bf, ln2_g, ln2_b, l):
    n = x1.shape[0]
    row_spec = pl.BlockSpec((TS_FIN, D_MODEL), lambda i: (i, 0))
    return pl.pallas_call(
        _final_kernel,
        grid=(n // TS_FIN,),
        in_specs=[
            row_spec, row_spec,
            pl.BlockSpec((D_MODEL // HALF, TS_FIN * SUBLANES, LANES), lambda i: (0, i, 0)),
            pl.BlockSpec((None, D_MODEL, 2 * D_SHARED), lambda i: (l, 0, 0)),
            pl.BlockSpec((None, D_SHARED, D_MODEL), lambda i: (l, 0, 0)),
            pl.BlockSpec((None, 1, D_MODEL), lambda i: (l, 0, 0)),
            pl.BlockSpec((None, 1, D_MODEL), lambda i: (l, 0, 0)),
        ],
        out_specs=[row_spec, row_spec],
        out_shape=[jax.ShapeDtypeStruct((n, D_MODEL), F32), jax.ShapeDtypeStruct((n, D_MODEL), BF16)],
        compiler_params=pltpu.CompilerParams(vmem_limit_bytes=VMEM_LIMIT),
        name="shared_ln2",
    )(x1, x1bf, routed, sh_gu_bf, sh_d_bf, ln2_g, ln2_b)


def kernel(x, mem, mem_ln_g, mem_ln_b, w_in, sgu_ln_g, sgu_ln_b, sgu_w, sgu_b, pool_w, pool_scale, conv_w, conv_b, conv_ln_g, conv_ln_b, w_mem_kv, w_out, ln1_g, ln1_b, w_router, router_bias, exp_w_gate, exp_w_up, exp_w_down, sh_w_gate, sh_w_up, sh_w_down, ln2_g, ln2_b):
    batch, seq, d = x.shape
    n = batch * seq
    depth = w_in.shape[0]
    assert (d, depth) == (D_MODEL, DEPTH) and seq % TS_MIX == 0 and n % BM_IN == 0 and n % T_ROUTE == 0

    w_in_p = jnp.concatenate([w_in[..., :2 * GW], w_in[..., 5 * GW:], w_in[..., 2 * GW:5 * GW]], axis=-1).astype(BF16)
    wout_bf = w_out.astype(BF16)
    sh_gu_bf = jnp.concatenate([sh_w_gate, sh_w_up], axis=-1).astype(BF16)
    sh_d_bf = sh_w_down.astype(BF16)
    w_router_t = jnp.swapaxes(w_router, 1, 2)
    bias3 = router_bias[..., None]

    def row(a):
        return a[:, None, :]

    p = dict(sgu_w=sgu_w, sgu_b=sgu_b, sgu_ln_g=row(sgu_ln_g), sgu_ln_b=row(sgu_ln_b), pool_w=pool_w,
             pool_scale=row(pool_scale), conv_w=conv_w, conv_b=row(conv_b), conv_ln_g=row(conv_ln_g),
             conv_ln_b=row(conv_ln_b), ln1_g=row(ln1_g), ln1_b=row(ln1_b))
    ln2_g3, ln2_b3 = row(ln2_g), row(ln2_b)

    kv = _memory_kv(mem.reshape(batch * N_MEM, d), mem_ln_g[None, :], mem_ln_b[None, :], w_mem_kv)

    nb = n * TOP_K // BM_EXP + N_EXPERTS
    assert (1 << BM_SHIFT) == BM_EXP

    xf = x.reshape(n, d)
    xbf = xf.astype(BF16)
    for l in range(depth):
        h6 = _in_proj(xbf, w_in_p, l)
        x1, x1bf, xpk = _mixers(h6, kv, xf, wout_bf, p, l, batch, seq)
        dest, wdense, emeta = _router(x1, w_router_t, bias3, l)
        slot_off = _slot_table(dest.reshape(-1), emeta, n, nb)
        bstart, nblk = emeta[2, :N_EXPERTS], emeta[3, :N_EXPERTS]
        hmid = _expert_gate_up(bstart, nblk, slot_off, xpk, wdense, exp_w_gate, exp_w_up, l)
        routed = _expert_down_combine(bstart, nblk, slot_off, hmid, exp_w_down, l, n)
        xf, xbf = _shared_ln2(x1, x1bf, routed, sh_gu_bf, sh_d_bf, ln2_g3, ln2_b3, l)
    return xf.reshape(batch, seq, d)
```

```python
import functools

import jax
import jax.numpy as jnp
from jax import lax
from jax.experimental import pallas as pl
from jax.experimental.pallas import tpu as pltpu

D_MODEL = 2048
DEPTH = 4
GW = 512
HEAD_DIM = 128
HEADS = 4
CHUNK = 128
POOL_WINDOWS = (2, 4, 8, 16)
CONV_WIDTH = 31
N_MEM = 256
N_EXPERTS = 64
N_GROUPS = 8
GROUP_SIZE = N_EXPERTS // N_GROUPS
TOPK_GROUPS = 4
TOP_K = 8
D_EXPERT = 256
D_SHARED = 256
ROUTED_SCALE = 2.5
LN_EPS = 1e-5
ALPHA = (2.0 * DEPTH) ** 0.25

LANES = 128
SUBLANES = 8
HALF = D_MODEL // 2
HALO = 32
VMEM_LIMIT = 56 * 1024 * 1024

TS_MIX = 256
BM_IN = 512
BN_IN = 1024
T_ROUTE = 512
BM_EXP = 256
BM_SHIFT = 8
TS_FIN = 256
SLOT_UNROLL = 16
H_SLOTS = 4

F32 = jnp.float32
BF16 = jnp.bfloat16
U32 = jnp.uint32
I32 = jnp.int32


def _ln(x, g, b):
    mu = jnp.mean(x, axis=-1, keepdims=True)
    xc = x - mu
    var = jnp.mean(xc * xc, axis=-1, keepdims=True)
    return xc * lax.rsqrt(var + LN_EPS) * g + b


def _dot(a, b):
    return jnp.dot(a, b, preferred_element_type=F32)


def _dot_nt(a, b):
    return lax.dot_general(a, b, (((1,), (1,)), ((), ())), preferred_element_type=F32)


def _kv_kernel(mem_ref, g_ref, b_ref, w_ref, o_ref):
    memn = _ln(mem_ref[...], g_ref[...], b_ref[...])
    o_ref[...] = _dot(memn.astype(BF16), w_ref[...].astype(BF16)).astype(BF16)


def _memory_kv(mem2, g, b, w_mem_kv):
    nb = mem2.shape[0] // N_MEM
    return pl.pallas_call(
        _kv_kernel,
        grid=(DEPTH, nb),
        in_specs=[
            pl.BlockSpec((N_MEM, D_MODEL), lambda l, i: (i, 0)),
            pl.BlockSpec((1, D_MODEL), lambda l, i: (0, 0)),
            pl.BlockSpec((1, D_MODEL), lambda l, i: (0, 0)),
            pl.BlockSpec((None, D_MODEL, 2 * GW), lambda l, i: (l, 0, 0)),
        ],
        out_specs=pl.BlockSpec((None, N_MEM, 2 * GW), lambda l, i: (l, i, 0)),
        out_shape=jax.ShapeDtypeStruct((DEPTH, mem2.shape[0], 2 * GW), BF16),
        compiler_params=pltpu.CompilerParams(vmem_limit_bytes=VMEM_LIMIT),
        name="memory_kv",
    )(mem2, g, b, w_mem_kv)


def _inproj_kernel(x_ref, w_ref, o_ref):
    acc = _dot(x_ref[...], w_ref[...])
    o_ref[0] = acc[:, :GW]
    o_ref[1] = acc[:, GW:]


def _in_proj(xbf, w_in_p, l):
    n = xbf.shape[0]
    return pl.pallas_call(
        _inproj_kernel,
        grid=(6 * GW // BN_IN, n // BM_IN),
        in_specs=[
            pl.BlockSpec((BM_IN, D_MODEL), lambda j, i: (i, 0)),
            pl.BlockSpec((None, D_MODEL, BN_IN), lambda j, i: (l, 0, j)),
        ],
        out_specs=pl.BlockSpec((BN_IN // GW, BM_IN, GW), lambda j, i: (j, i, 0)),
        out_shape=jax.ShapeDtypeStruct((6, n, GW), F32),
        compiler_params=pltpu.CompilerParams(vmem_limit_bytes=VMEM_LIMIT),
        name="in_proj",
    )(xbf, w_in_p)


def _pack_tokens(x, xpk_ref):
    rows = x.shape[0]
    for s in range(SUBLANES):
        lo = x[:, s * LANES:(s + 1) * LANES]
        hi = x[:, HALF + s * LANES:HALF + (s + 1) * LANES]
        xpk_ref[pl.ds(s, rows, stride=SUBLANES), :] = pltpu.pack_elementwise([lo, hi], packed_dtype=BF16)


def _unpack_tokens(word):
    lo = pltpu.unpack_elementwise(word, index=0, packed_dtype=BF16, unpacked_dtype=F32)
    hi = pltpu.unpack_elementwise(word, index=1, packed_dtype=BF16, unpacked_dtype=F32)
    return lo.astype(BF16), hi.astype(BF16)


def _mix_kernel(h_ref, halo_ref, kv_ref, x_ref, wout_ref, sguw_ref, sgub_ref, sgug_ref, sgubeta_ref,
                poolw_ref, pools_ref, convw_ref, convb_ref, cvg_ref, cvb_ref, ln1g_ref, ln1b_ref,
                x1_ref, x1bf_ref, xpk_ref, mix_ref, cbuf_ref):
    i = pl.program_id(1)
    ts = h_ref.shape[1]
    first = i == 0

    u = jax.nn.gelu(h_ref[0])
    v = jax.nn.gelu(h_ref[1])
    tri = (lax.broadcasted_iota(I32, (CHUNK, CHUNK), 0) >= lax.broadcasted_iota(I32, (CHUNK, CHUNK), 1))
    for hd in range(HEADS):
        cs = slice(hd * HEAD_DIM, (hd + 1) * HEAD_DIM)
        vn = _ln(v[:, cs], sgug_ref[:, cs], sgubeta_ref[:, cs]).astype(BF16)
        wm = jnp.where(tri, sguw_ref[hd], 0.0).astype(BF16)
        bcol = sgub_ref[:, hd:hd + 1]
        for c in range(ts // CHUNK):
            rs = slice(c * CHUNK, (c + 1) * CHUNK)
            mixed = _dot(wm, vn[rs]) + bcol
            mix_ref[rs, cs] = (u[rs, cs] * mixed).astype(BF16)

    hp = h_ref[3]
    halo_p = jnp.where(first, 0.0, halo_ref[0])
    ext = jnp.concatenate([halo_p, hp], axis=0)
    pos1 = (i * ts + lax.broadcasted_iota(I32, (ts, 1), 0) + 1).astype(F32)
    for g, win in enumerate(POOL_WINDOWS):
        cs = slice(g * LANES, (g + 1) * LANES)
        s = ext[:, cs]
        sh = 1
        while sh < win:
            s = s + pltpu.roll(s, sh, 0)
            sh *= 2
        cnt = jnp.minimum(pos1, float(win))
        pooled = s[HALO:] / cnt - hp[:, cs]
        y = _dot(pooled.astype(BF16), poolw_ref[g].astype(BF16)) * pools_ref[:, cs]
        mix_ref[:, GW + g * LANES:GW + (g + 1) * LANES] = y.astype(BF16)

    glu_ext = jnp.concatenate([jnp.where(first, 0.0, halo_ref[1] * jax.nn.sigmoid(halo_ref[2])),
                               h_ref[4] * jax.nn.sigmoid(h_ref[5])], axis=0)
    cbuf_ref[0] = glu_ext
    for r in range(1, SUBLANES):
        cbuf_ref[r] = pltpu.roll(glu_ext, ts + HALO - r, 0)
    rchunk = 64
    for g in range(GW // LANES):
        cs = slice(g * LANES, (g + 1) * LANES)
        parts = []
        for r0 in range(0, ts, rchunk):
            acc = jnp.zeros((rchunk, LANES), F32)
            for j in range(CONV_WIDTH):
                lead = HALO - (CONV_WIDTH - 1) + j
                start = r0 + lead - lead % SUBLANES
                acc = acc + convw_ref[j:j + 1, cs] * cbuf_ref[lead % SUBLANES, start:start + rchunk, cs]
            parts.append(acc)
        y = jnp.concatenate(parts, axis=0) + convb_ref[:, cs]
        y = _ln(y, cvg_ref[:, cs], cvb_ref[:, cs])
        mix_ref[:, 2 * GW + g * LANES:2 * GW + (g + 1) * LANES] = jax.nn.silu(y).astype(BF16)

    q = h_ref[2]
    for hd in range(HEADS):
        cs = slice(hd * HEAD_DIM, (hd + 1) * HEAD_DIM)
        kh = kv_ref[:, hd * HEAD_DIM:(hd + 1) * HEAD_DIM]
        vh = kv_ref[:, GW + hd * HEAD_DIM:GW + (hd + 1) * HEAD_DIM]
        sc = _dot_nt(q[:, cs].astype(BF16), kh) * (HEAD_DIM ** -0.5)
        e = jnp.exp(sc - jnp.max(sc, axis=-1, keepdims=True))
        pr = e * (1.0 / jnp.sum(e, axis=-1, keepdims=True))
        mix_ref[:, 3 * GW + hd * HEAD_DIM:3 * GW + (hd + 1) * HEAD_DIM] = _dot(pr.astype(BF16), vh).astype(BF16)

    z = ALPHA * x_ref[...] + _dot(mix_ref[...], wout_ref[...])
    x1 = _ln(z, ln1g_ref[...], ln1b_ref[...])
    x1_ref[...] = x1
    x1bf_ref[...] = x1.astype(BF16)
    _pack_tokens(x1, xpk_ref)


def _mixers(h6, kv, x, wout_bf, p, l, batch, seq):
    n = x.shape[0]
    ns = seq // TS_MIX
    hb = TS_MIX // HALO

    def vec(width):
        return pl.BlockSpec((None, 1, width), lambda b, i: (l, 0, 0))

    return pl.pallas_call(
        _mix_kernel,
        grid=(batch, ns),
        in_specs=[
            pl.BlockSpec((6, TS_MIX, GW), lambda b, i: (0, b * ns + i, 0)),
            pl.BlockSpec((3, HALO, GW), lambda b, i: (1, jnp.maximum((b * ns + i) * hb - 1, 0), 0)),
            pl.BlockSpec((None, N_MEM, 2 * GW), lambda b, i: (l, b, 0)),
            pl.BlockSpec((TS_MIX, D_MODEL), lambda b, i: (b * ns + i, 0)),
            pl.BlockSpec((None, D_MODEL, D_MODEL), lambda b, i: (l, 0, 0)),
            pl.BlockSpec((None, HEADS, CHUNK, CHUNK), lambda b, i: (l, 0, 0, 0)),
            pl.BlockSpec((None, CHUNK, HEADS), lambda b, i: (l, 0, 0)),
            vec(GW), vec(GW),
            pl.BlockSpec((None, len(POOL_WINDOWS), LANES, LANES), lambda b, i: (l, 0, 0, 0)),
            vec(GW),
            pl.BlockSpec((None, CONV_WIDTH, GW), lambda b, i: (l, 0, 0)),
            vec(GW), vec(GW), vec(GW), vec(D_MODEL), vec(D_MODEL),
        ],
        out_specs=[
            pl.BlockSpec((TS_MIX, D_MODEL), lambda b, i: (b * ns + i, 0)),
            pl.BlockSpec((TS_MIX, D_MODEL), lambda b, i: (b * ns + i, 0)),
            pl.BlockSpec((TS_MIX * SUBLANES, LANES), lambda b, i: (b * ns + i, 0)),
        ],
        out_shape=[
            jax.ShapeDtypeStruct((n, D_MODEL), F32),
            jax.ShapeDtypeStruct((n, D_MODEL), BF16),
            jax.ShapeDtypeStruct((n * SUBLANES, LANES), U32),
        ],
        scratch_shapes=[
            pltpu.VMEM((TS_MIX, D_MODEL), BF16),
            pltpu.VMEM((SUBLANES, TS_MIX + HALO, GW), F32),
        ],
        compiler_params=pltpu.CompilerParams(vmem_limit_bytes=VMEM_LIMIT),
        name="mixers_outproj_ln1",
    )(h6, h6, kv, x, wout_bf, p['sgu_w'], p['sgu_b'], p['sgu_ln_g'], p['sgu_ln_b'],
      p['pool_w'], p['pool_scale'], p['conv_w'], p['conv_b'], p['conv_ln_g'], p['conv_ln_b'],
      p['ln1_g'], p['ln1_b'])


def _split_bf16(a):
    hi = a.astype(BF16)
    lo = (a - hi.astype(F32)).astype(BF16)
    return hi, lo


def _router_kernel(x_ref, wrt_ref, bias_ref, dest_ref, wdense_ref, emeta_ref, carry_ref, idx_ref):
    t = x_ref.shape[0]
    neg = -jnp.inf
    step = pl.program_id(0)

    @pl.when(step == 0)
    def _():
        carry_ref[...] = jnp.zeros_like(carry_ref)

    xh, xl = _split_bf16(x_ref[...])
    wh, wl = _split_bf16(wrt_ref[...])
    logits = _dot_nt(wh, xh) + (_dot_nt(wh, xl) + _dot_nt(wl, xh))
    scores = jax.nn.sigmoid(logits)
    choice = scores + bias_ref[...]

    iota8 = lax.broadcasted_iota(I32, (GROUP_SIZE, t), 0)
    rows = []
    for g in range(N_GROUPS):
        cg = choice[g * GROUP_SIZE:(g + 1) * GROUP_SIZE]
        m1 = jnp.max(cg, axis=0, keepdims=True)
        i1 = jnp.min(jnp.where(cg == m1, iota8, GROUP_SIZE), axis=0, keepdims=True)
        m2 = jnp.max(jnp.where(iota8 == i1, neg, cg), axis=0, keepdims=True)
        rows.append(m1 + m2)
    gwork = jnp.concatenate(rows, axis=0)

    gsel = jnp.zeros((N_GROUPS, t), jnp.bool_)
    for _ in range(TOPK_GROUPS):
        m = jnp.max(gwork, axis=0, keepdims=True)
        ii = jnp.min(jnp.where(gwork == m, iota8, N_GROUPS), axis=0, keepdims=True)
        hit = iota8 == ii
        gsel = gsel | hit
        gwork = jnp.where(hit, neg, gwork)
    emask = jnp.concatenate(
        [jnp.broadcast_to(gsel[g:g + 1], (GROUP_SIZE, t)) for g in range(N_GROUPS)], axis=0)

    iota_e = lax.broadcasted_iota(I32, (N_EXPERTS, t), 0)
    work = jnp.where(emask, choice, neg)
    sel = jnp.zeros((N_EXPERTS, t), jnp.bool_)
    idx_rows, raw_rows = [], []
    for _ in range(TOP_K):
        m = jnp.max(work, axis=0, keepdims=True)
        ii = jnp.min(jnp.where(work == m, iota_e, N_EXPERTS), axis=0, keepdims=True)
        hit = iota_e == ii
        idx_rows.append(ii)
        raw_rows.append(jnp.sum(jnp.where(hit, scores, 0.0), axis=0, keepdims=True))
        sel = sel | hit
        work = jnp.where(hit, neg, work)
    wsum = raw_rows[0]
    for r in raw_rows[1:]:
        wsum = wsum + r
    scale = ROUTED_SCALE / (wsum + 1e-20)

    dense = jnp.zeros((N_EXPERTS, t), F32)
    for k in range(TOP_K):
        dense = jnp.where(iota_e == idx_rows[k], raw_rows[k] * scale, dense)
    wdense_ref[...] = jnp.concatenate([dense, jnp.zeros((LANES - N_EXPERTS, t), F32)], axis=0).T

    upper = (lax.broadcasted_iota(I32, (t, t), 0) < lax.broadcasted_iota(I32, (t, t), 1))
    sel_f = sel.astype(F32)
    prefix = _dot(sel_f.astype(BF16), upper.astype(F32).astype(BF16)) + carry_ref[...]
    pos_rows = [jnp.sum(jnp.where(iota_e == idx_rows[k], prefix, 0.0), axis=0, keepdims=True)
                for k in range(TOP_K)]
    dest_ref[step] = jnp.concatenate(pos_rows, axis=0).astype(I32)
    idx_ref[step] = jnp.concatenate(idx_rows, axis=0)
    carry = carry_ref[...] + jnp.sum(sel_f, axis=1, keepdims=True)
    carry_ref[...] = carry

    @pl.when(step == pl.num_programs(0) - 1)
    def _():
        cnt = carry.astype(I32)
        nblk = (cnt + (BM_EXP - 1)) >> BM_SHIFT
        lower = (lax.broadcasted_iota(I32, (N_EXPERTS, N_EXPERTS), 0)
                 >= lax.broadcasted_iota(I32, (N_EXPERTS, N_EXPERTS), 1))
        nblk_wide = jnp.broadcast_to(nblk.astype(F32), (N_EXPERTS, LANES)).astype(BF16)
        bend = _dot(lower.astype(F32).astype(BF16), nblk_wide)[:, 0:1].astype(I32)
        bstart = bend - nblk
        pad_start = bstart << BM_SHIFT
        pad_rows = jnp.broadcast_to(pad_start, (N_EXPERTS, t))

        def add_start(tile, c):
            idx_t = idx_ref[tile]
            add = jnp.zeros((TOP_K, t), I32)
            for ex in range(N_EXPERTS):
                add = jnp.where(idx_t == ex, pad_rows[ex:ex + 1], add)
            dest_ref[tile] = dest_ref[tile] + add
            return c

        lax.fori_loop(0, dest_ref.shape[0], add_start, 0)

        eye = (lax.broadcasted_iota(I32, (N_EXPERTS, LANES), 0) == lax.broadcasted_iota(I32, (N_EXPERTS, LANES), 1))

        def as_row(col):
            return jnp.sum(jnp.where(eye, col, 0), axis=0, keepdims=True)

        emeta_ref[...] = jnp.concatenate(
            [as_row(pad_start + cnt), as_row(bend << BM_SHIFT), as_row(bstart), as_row(nblk),
             jnp.zeros((SUBLANES - 4, LANES), I32)], axis=0)


def _router(x1, w_router_t, bias, l):
    n = x1.shape[0]
    nt = n // T_ROUTE
    return pl.pallas_call(
        _router_kernel,
        grid=(nt,),
        in_specs=[
            pl.BlockSpec((T_ROUTE, D_MODEL), lambda t: (t, 0)),
            pl.BlockSpec((None, N_EXPERTS, D_MODEL), lambda t: (l, 0, 0)),
            pl.BlockSpec((None, N_EXPERTS, 1), lambda t: (l, 0, 0)),
        ],
        out_specs=[
            pl.BlockSpec((nt, TOP_K, T_ROUTE), lambda t: (0, 0, 0)),
            pl.BlockSpec((T_ROUTE, LANES), lambda t: (t, 0)),
            pl.BlockSpec((SUBLANES, LANES), lambda t: (0, 0)),
        ],
        out_shape=[
            jax.ShapeDtypeStruct((nt, TOP_K, T_ROUTE), I32),
            jax.ShapeDtypeStruct((n, LANES), F32),
            jax.ShapeDtypeStruct((SUBLANES, LANES), I32),
        ],
        scratch_shapes=[pltpu.VMEM((N_EXPERTS, 1), F32), pltpu.VMEM((nt, TOP_K, T_ROUTE), I32)],
        compiler_params=pltpu.CompilerParams(vmem_limit_bytes=VMEM_LIMIT),
        name="router",
    )(x1, w_router_t, bias)


def _slot_kernel(dest_ref, emeta_ref, off_ref, *, n_tok):
    spare = n_tok * SUBLANES
    group = SUBLANES

    def fill_range(start, end):
        def fill(j, c):
            for u in range(group):
                off_ref[start + j * group + u] = spare
            return c
        lax.fori_loop(0, (end - start + group - 1) // group, fill, 0)

    def fill_expert(ex, c):
        fill_range(emeta_ref[0, ex], emeta_ref[1, ex])
        return c

    lax.fori_loop(0, N_EXPERTS, fill_expert, 0)
    fill_range(emeta_ref[1, N_EXPERTS - 1], off_ref.shape[0])

    def tile_body(tile, c):
        for k in range(TOP_K):
            base = (tile * TOP_K + k) * T_ROUTE

            def inner(j, c2, base=base):
                src = base + j * SLOT_UNROLL
                val = (tile * T_ROUTE + j * SLOT_UNROLL) * SUBLANES
                for u in range(SLOT_UNROLL):
                    off_ref[dest_ref[src + u]] = val + u * SUBLANES
                return c2

            lax.fori_loop(0, T_ROUTE // SLOT_UNROLL, inner, 0)
        return c

    lax.fori_loop(0, n_tok // T_ROUTE, tile_body, 0)


def _slot_table(dest_flat, emeta, n, nb):
    smem = pl.BlockSpec(memory_space=pltpu.SMEM)
    return pl.pallas_call(
        functools.partial(_slot_kernel, n_tok=n),
        in_specs=[smem, smem],
        out_specs=smem,
        out_shape=jax.ShapeDtypeStruct(((nb + 1) * BM_EXP,), I32),
        name="slot_table",
    )(dest_flat, emeta)


def _block_rows(first_block, j):
    return pl.ds(pl.multiple_of((first_block + j) * BM_EXP, BM_EXP), BM_EXP)


def _e1_kernel(bstart_ref, nblk_ref, off_ref, xpk_ref, wdense_ref, wg_ref, wu_ref, h_hbm,
               stage_ref, wrow_ref, lhs_ref, wcat_ref, hbuf_ref, sem):
    e = pl.program_id(0)
    nblk = nblk_ref[e]
    first = bstart_ref[e]
    row_mask = xpk_ref.shape[0] - 1

    def h_copy(g):
        return pltpu.make_async_copy(hbuf_ref.at[g & 1], h_hbm.at[_block_rows(0, g)], sem.at[g & 1])

    @pl.when(nblk > 0)
    def _():
        wcat_ref[:, :D_EXPERT] = wg_ref[...].astype(BF16)
        wcat_ref[:, D_EXPERT:] = wu_ref[...].astype(BF16)
        lane = lax.broadcasted_iota(I32, (BM_EXP, LANES), 1)

        def block(j, carry):
            g = first + j
            slot = g & 1
            base = g * BM_EXP

            @pl.when(g >= 2)
            def _():
                h_copy(g - 2).wait()

            for r in range(BM_EXP):
                off = off_ref[base + r] & row_mask
                stage_ref[r * SUBLANES:(r + 1) * SUBLANES, :] = xpk_ref[pl.ds(pl.multiple_of(off, SUBLANES), SUBLANES), :]
                wrow_ref[r:r + 1, :] = wdense_ref[pl.ds(off >> 3, 1), :]
            wcol = jnp.sum(jnp.where(lane == e, wrow_ref[...], 0.0), axis=1, keepdims=True)
            for s in range(SUBLANES):
                lo, hi = _unpack_tokens(stage_ref[pl.ds(s, BM_EXP, stride=SUBLANES), :])
                lhs_ref[:, s * LANES:(s + 1) * LANES] = lo
                lhs_ref[:, HALF + s * LANES:HALF + (s + 1) * LANES] = hi
            gu = _dot(lhs_ref[...], wcat_ref[...])
            hbuf_ref[slot] = (jax.nn.silu(gu[:, :D_EXPERT]) * gu[:, D_EXPERT:] * wcol).astype(BF16)
            h_copy(g).start()
            return carry

        lax.fori_loop(0, nblk, block, 0)

    @pl.when(e == N_EXPERTS - 1)
    def _():
        total = first + nblk

        @pl.when(total >= 2)
        def _():
            h_copy(total - 2).wait()

        @pl.when(total >= 1)
        def _():
            h_copy(total - 1).wait()

        hbuf_ref[0] = jnp.zeros((BM_EXP, D_EXPERT), BF16)

        def zero_block(j, carry):
            copy = pltpu.make_async_copy(hbuf_ref.at[0], h_hbm.at[_block_rows(0, j)], sem.at[0])
            copy.start()
            copy.wait()
            return carry

        lax.fori_loop(first + nblk, h_hbm.shape[0] // BM_EXP, zero_block, 0)


def _expert_gate_up(bstart, nblk, slot_off, xpk, wdense, exp_w_gate, exp_w_up, l):
    assert xpk.shape[0] & (xpk.shape[0] - 1) == 0
    grid_spec = pltpu.PrefetchScalarGridSpec(
        num_scalar_prefetch=3,
        grid=(N_EXPERTS,),
        in_specs=[
            pl.BlockSpec(xpk.shape, lambda e, *_: (0, 0), pipeline_mode=pl.Buffered(1)),
            pl.BlockSpec(wdense.shape, lambda e, *_: (0, 0), pipeline_mode=pl.Buffered(1)),
            pl.BlockSpec((None, None, D_MODEL, D_EXPERT), lambda e, *_: (l, e, 0, 0)),
            pl.BlockSpec((None, None, D_MODEL, D_EXPERT), lambda e, *_: (l, e, 0, 0)),
        ],
        out_specs=pl.BlockSpec(memory_space=pl.ANY),
        scratch_shapes=[
            pltpu.VMEM((BM_EXP * SUBLANES, LANES), U32),
            pltpu.VMEM((BM_EXP, LANES), F32),
            pltpu.VMEM((BM_EXP, D_MODEL), BF16),
            pltpu.VMEM((D_MODEL, 2 * D_EXPERT), BF16),
            pltpu.VMEM((2, BM_EXP, D_EXPERT), BF16),
            pltpu.SemaphoreType.DMA((2,)),
        ],
    )
    return pl.pallas_call(
        _e1_kernel,
        grid_spec=grid_spec,
        out_shape=jax.ShapeDtypeStruct((slot_off.shape[0], D_EXPERT), BF16),
        compiler_params=pltpu.CompilerParams(vmem_limit_bytes=VMEM_LIMIT),
        name="expert_gate_up",
    )(bstart, nblk, slot_off, xpk, wdense, exp_w_gate, exp_w_up)


def _e2_kernel(bstart_ref, nblk_ref, off_ref, h_hbm, wd_ref, out_ref, ybuf_ref, hbuf_ref, wdb_ref, sem):
    e = pl.program_id(1)
    nblk = nblk_ref[e]
    first = bstart_ref[e]
    total = bstart_ref[N_EXPERTS - 1] + nblk_ref[N_EXPERTS - 1]
    group = SUBLANES
    chunks = HALF // LANES

    def h_copy(g):
        slot = g % H_SLOTS
        return pltpu.make_async_copy(h_hbm.at[_block_rows(0, g)], hbuf_ref.at[slot], sem.at[slot])

    @pl.when(e == 0)
    def _():
        out_ref[...] = jnp.zeros_like(out_ref)
        for g in range(H_SLOTS):
            @pl.when(g < total)
            def _(g=g):
                h_copy(g).start()

    @pl.when(nblk > 0)
    def _():
        wdb_ref[...] = wd_ref[...].astype(BF16)

        def block(j, carry):
            g = first + j
            slot = g % H_SLOTS
            base = g * BM_EXP
            h_copy(g).wait()

            y = _dot(hbuf_ref[slot], wdb_ref[...])
            for t in range(BM_EXP // SUBLANES):
                for c in range(chunks):
                    t0 = (t * chunks + c) * SUBLANES
                    ybuf_ref[t0:t0 + SUBLANES, :] = y[t * SUBLANES:(t + 1) * SUBLANES, c * LANES:(c + 1) * LANES]

            @pl.when(g + H_SLOTS < total)
            def _():
                h_copy(g + H_SLOTS).start()

            for g0 in range(0, BM_EXP, group):
                dsts, vals = [], []
                for r in range(g0, g0 + group):
                    dst = pl.ds(pl.multiple_of(off_ref[base + r], SUBLANES), SUBLANES)
                    row = ybuf_ref[pl.ds((r // SUBLANES) * (HALF // LANES) * SUBLANES + r % SUBLANES,
                                         HALF // LANES, stride=SUBLANES), :]
                    dsts.append(dst)
                    vals.append(out_ref[dst, :] + row)
                for dst, val in zip(dsts, vals):
                    out_ref[dst, :] = val
            return carry

        lax.fori_loop(0, nblk, block, 0)


def _expert_down_combine(bstart, nblk, slot_off, hmid, exp_w_down, l, n):
    rows = (n + 1) * SUBLANES
    grid_spec = pltpu.PrefetchScalarGridSpec(
        num_scalar_prefetch=3,
        grid=(D_MODEL // HALF, N_EXPERTS),
        in_specs=[
            pl.BlockSpec(memory_space=pl.ANY),
            pl.BlockSpec((None, None, D_EXPERT, HALF), lambda c, e, *_: (l, e, 0, c)),
        ],
        out_specs=pl.BlockSpec((None, rows, LANES), lambda c, e, *_: (c, 0, 0),
                               pipeline_mode=pl.Buffered(1)),
        scratch_shapes=[
            pltpu.VMEM((BM_EXP * HALF // LANES, LANES), F32),
            pltpu.VMEM((H_SLOTS, BM_EXP, D_EXPERT), BF16),
            pltpu.VMEM((D_EXPERT, HALF), BF16),
            pltpu.SemaphoreType.DMA((H_SLOTS,)),
        ],
    )
    return pl.pallas_call(
        _e2_kernel,
        grid_spec=grid_spec,
        out_shape=jax.ShapeDtypeStruct((D_MODEL // HALF, rows, LANES), F32),
        compiler_params=pltpu.CompilerParams(vmem_limit_bytes=VMEM_LIMIT),
        name="expert_down_combine",
    )(bstart, nblk, slot_off, hmid, exp_w_down)


def _final_kernel(x1_ref, x1bf_ref, routed_ref, wgu_ref, wd_ref, g_ref, b_ref, x2_ref, x2bf_ref):
    ts = x1_ref.shape[0]
    gu = _dot(x1bf_ref[...], wgu_ref[...])
    hs = (jax.nn.silu(gu[:, :D_SHARED]) * gu[:, D_SHARED:]).astype(BF16)
    shared = _dot(hs, wd_ref[...])
    routed = jnp.concatenate(
        [routed_ref[c, pl.ds(s, ts, stride=SUBLANES), :]
         for c in range(D_MODEL // HALF) for s in range(SUBLANES)], axis=1)
    z = ALPHA * x1_ref[...] + (shared + routed)
    x2 = _ln(z, g_ref[...], b_ref[...])
    x2_ref[...] = x2
    x2bf_ref[...] = x2.astype(BF16)


def _shared_ln2(x1, x1bf, routed, sh_gu_bf, sh_d_bf, ln2_g, ln2_b, l):
    n = x1.shape[0]
    row_spec = pl.BlockSpec((TS_FIN, D_MODEL), lambda i: (i, 0))
    return pl.pallas_call(
        _final_kernel,
        grid=(n // TS_FIN,),
        in_specs=[
            row_spec, row_spec,
            pl.BlockSpec((D_MODEL // HALF, TS_FIN * SUBLANES, LANES), lambda i: (0, i, 0)),
            pl.BlockSpec((None, D_MODEL, 2 * D_SHARED), lambda i: (l, 0, 0)),
            pl.BlockSpec((None, D_SHARED, D_MODEL), lambda i: (l, 0, 0)),
            pl.BlockSpec((None, 1, D_MODEL), lambda i: (l, 0, 0)),
            pl.BlockSpec((None, 1, D_MODEL), lambda i: (l, 0, 0)),
        ],
        out_specs=[row_spec, row_spec],
        out_shape=[jax.ShapeDtypeStruct((n, D_MODEL), F32), jax.ShapeDtypeStruct((n, D_MODEL), BF16)],
        compiler_params=pltpu.CompilerParams(vmem_limit_bytes=VMEM_LIMIT),
        name="shared_ln2",
    )(x1, x1bf, routed, sh_gu_bf, sh_d_bf, ln2_g, ln2_b)


def kernel(x, mem, mem_ln_g, mem_ln_b, w_in, sgu_ln_g, sgu_ln_b, sgu_w, sgu_b, pool_w, pool_scale, conv_w, conv_b, conv_ln_g, conv_ln_b, w_mem_kv, w_out, ln1_g, ln1_b, w_router, router_bias, exp_w_gate, exp_w_up, exp_w_down, sh_w_gate, sh_w_up, sh_w_down, ln2_g, ln2_b):
    batch, seq, d = x.shape
    n = batch * seq
    depth = w_in.shape[0]
    assert (d, depth) == (D_MODEL, DEPTH) and seq % TS_MIX == 0 and n % BM_IN == 0 and n % T_ROUTE == 0

    w_in_p = jnp.concatenate([w_in[..., :2 * GW], w_in[..., 5 * GW:], w_in[..., 2 * GW:5 * GW]], axis=-1).astype(BF16)
    wout_bf = w_out.astype(BF16)
    sh_gu_bf = jnp.concatenate([sh_w_gate, sh_w_up], axis=-1).astype(BF16)
    sh_d_bf = sh_w_down.astype(BF16)
    w_router_t = jnp.swapaxes(w_router, 1, 2)
    bias3 = router_bias[..., None]

    def row(a):
        return a[:, None, :]

    p = dict(sgu_w=sgu_w, sgu_b=sgu_b, sgu_ln_g=row(sgu_ln_g), sgu_ln_b=row(sgu_ln_b), pool_w=pool_w,
             pool_scale=row(pool_scale), conv_w=conv_w, conv_b=row(conv_b), conv_ln_g=row(conv_ln_g),
             conv_ln_b=row(conv_ln_b), ln1_g=row(ln1_g), ln1_b=row(ln1_b))
    ln2_g3, ln2_b3 = row(ln2_g), row(ln2_b)

    kv = _memory_kv(mem.reshape(batch * N_MEM, d), mem_ln_g[None, :], mem_ln_b[None, :], w_mem_kv)

    nb = n * TOP_K // BM_EXP + N_EXPERTS
    assert (1 << BM_SHIFT) == BM_EXP

    xf = x.reshape(n, d)
    xbf = xf.astype(BF16)
    for l in range(depth):
        h6 = _in_proj(xbf, w_in_p, l)
        x1, x1bf, xpk = _mixers(h6, kv, xf, wout_bf, p, l, batch, seq)
        dest, wdense, emeta = _router(x1, w_router_t, bias3, l)
        slot_off = _slot_table(dest.reshape(-1), emeta, n, nb)
        bstart, nblk = emeta[2, :N_EXPERTS], emeta[3, :N_EXPERTS]
        hmid = _expert_gate_up(bstart, nblk, slot_off, xpk, wdense, exp_w_gate, exp_w_up, l)
        routed = _expert_down_combine(bstart, nblk, slot_off, hmid, exp_w_down, l, n)
        xf, xbf = _shared_ln2(x1, x1bf, routed, sh_gu_bf, sh_d_bf, ln2_g3, ln2_b3, l)
    return xf.reshape(batch, seq, d)
```

```python
import functools

import jax
import jax.numpy as jnp
from jax import lax
from jax.experimental import pallas as pl
from jax.experimental.pallas import tpu as pltpu

D_MODEL = 2048
DEPTH = 4
GW = 512
HEAD_DIM = 128
HEADS = 4
CHUNK = 128
POOL_WINDOWS = (2, 4, 8, 16)
CONV_WIDTH = 31
N_MEM = 256
N_EXPERTS = 64
N_GROUPS = 8
GROUP_SIZE = N_EXPERTS // N_GROUPS
TOPK_GROUPS = 4
TOP_K = 8
D_EXPERT = 256
D_SHARED = 256
ROUTED_SCALE = 2.5
LN_EPS = 1e-5
ALPHA = (2.0 * DEPTH) ** 0.25

LANES = 128
SUBLANES = 8
HALF = D_MODEL // 2
HALO = 32
VMEM_LIMIT = 56 * 1024 * 1024

TS_MIX = 256
BM_IN = 512
BN_IN = 1024
T_ROUTE = 512
BM_EXP = 256
BM_SHIFT = 8
TS_FIN = 256
SLOT_UNROLL = 16
H_SLOTS = 4

F32 = jnp.float32
BF16 = jnp.bfloat16
U32 = jnp.uint32
I32 = jnp.int32


def _ln(x, g, b):
    mu = jnp.mean(x, axis=-1, keepdims=True)
    xc = x - mu
    var = jnp.mean(xc * xc, axis=-1, keepdims=True)
    return xc * lax.rsqrt(var + LN_EPS) * g + b


def _dot(a, b):
    return jnp.dot(a, b, preferred_element_type=F32)


def _dot_nt(a, b):
    return lax.dot_general(a, b, (((1,), (1,)), ((), ())), preferred_element_type=F32)


def _kv_kernel(mem_ref, g_ref, b_ref, w_ref, o_ref):
    memn = _ln(mem_ref[...], g_ref[...], b_ref[...])
    o_ref[...] = _dot(memn.astype(BF16), w_ref[...].astype(BF16)).astype(BF16)


def _memory_kv(mem2, g, b, w_mem_kv):
    nb = mem2.shape[0] // N_MEM
    return pl.pallas_call(
        _kv_kernel,
        grid=(DEPTH, nb),
        in_specs=[
            pl.BlockSpec((N_MEM, D_MODEL), lambda l, i: (i, 0)),
            pl.BlockSpec((1, D_MODEL), lambda l, i: (0, 0)),
            pl.BlockSpec((1, D_MODEL), lambda l, i: (0, 0)),
            pl.BlockSpec((None, D_MODEL, 2 * GW), lambda l, i: (l, 0, 0)),
        ],
        out_specs=pl.BlockSpec((None, N_MEM, 2 * GW), lambda l, i: (l, i, 0)),
        out_shape=jax.ShapeDtypeStruct((DEPTH, mem2.shape[0], 2 * GW), BF16),
        compiler_params=pltpu.CompilerParams(vmem_limit_bytes=VMEM_LIMIT),
        name="memory_kv",
    )(mem2, g, b, w_mem_kv)


def _inproj_kernel(x_ref, w_ref, o_ref):
    acc = _dot(x_ref[...], w_ref[...])
    o_ref[0] = acc[:, :GW]
    o_ref[1] = acc[:, GW:]


def _in_proj(xbf, w_in_p, l):
    n = xbf.shape[0]
    return pl.pallas_call(
        _inproj_kernel,
        grid=(6 * GW // BN_IN, n // BM_IN),
        in_specs=[
            pl.BlockSpec((BM_IN, D_MODEL), lambda j, i: (i, 0)),
            pl.BlockSpec((None, D_MODEL, BN_IN), lambda j, i: (l, 0, j)),
        ],
        out_specs=pl.BlockSpec((BN_IN // GW, BM_IN, GW), lambda j, i: (j, i, 0)),
        out_shape=jax.ShapeDtypeStruct((6, n, GW), F32),
        compiler_params=pltpu.CompilerParams(vmem_limit_bytes=VMEM_LIMIT),
        name="in_proj",
    )(xbf, w_in_p)


def _pack_tokens(x, xpk_ref):
    rows = x.shape[0]
    for s in range(SUBLANES):
        lo = x[:, s * LANES:(s + 1) * LANES]
        hi = x[:, HALF + s * LANES:HALF + (s + 1) * LANES]
        xpk_ref[pl.ds(s, rows, stride=SUBLANES), :] = pltpu.pack_elementwise([lo, hi], packed_dtype=BF16)


def _unpack_tokens(word):
    lo = pltpu.unpack_elementwise(word, index=0, packed_dtype=BF16, unpacked_dtype=F32)
    hi = pltpu.unpack_elementwise(word, index=1, packed_dtype=BF16, unpacked_dtype=F32)
    return lo.astype(BF16), hi.astype(BF16)


def _mix_kernel(h_ref, halo_ref, kv_ref, x_ref, wout_ref, sguw_ref, sgub_ref, sgug_ref, sgubeta_ref,
                poolw_ref, pools_ref, convw_ref, convb_ref, cvg_ref, cvb_ref, ln1g_ref, ln1b_ref,
                x1_ref, x1bf_ref, xpk_ref, mix_ref, cbuf_ref):
    i = pl.program_id(1)
    ts = h_ref.shape[1]
    first = i == 0

    u = jax.nn.gelu(h_ref[0])
    v = jax.nn.gelu(h_ref[1])
    tri = (lax.broadcasted_iota(I32, (CHUNK, CHUNK), 0) >= lax.broadcasted_iota(I32, (CHUNK, CHUNK), 1))
    for hd in range(HEADS):
        cs = slice(hd * HEAD_DIM, (hd + 1) * HEAD_DIM)
        vn = _ln(v[:, cs], sgug_ref[:, cs], sgubeta_ref[:, cs]).astype(BF16)
        wm = jnp.where(tri, sguw_ref[hd], 0.0).astype(BF16)
        bcol = sgub_ref[:, hd:hd + 1]
        for c in range(ts // CHUNK):
            rs = slice(c * CHUNK, (c + 1) * CHUNK)
            mixed = _dot(wm, vn[rs]) + bcol
            mix_ref[rs, cs] = (u[rs, cs] * mixed).astype(BF16)

    hp = h_ref[3]
    halo_p = jnp.where(first, 0.0, halo_ref[0])
    ext = jnp.concatenate([halo_p, hp], axis=0)
    pos1 = (i * ts + lax.broadcasted_iota(I32, (ts, 1), 0) + 1).astype(F32)
    for g, win in enumerate(POOL_WINDOWS):
        cs = slice(g * LANES, (g + 1) * LANES)
        s = ext[:, cs]
        sh = 1
        while sh < win:
            s = s + pltpu.roll(s, sh, 0)
            sh *= 2
        cnt = jnp.minimum(pos1, float(win))
        pooled = s[HALO:] / cnt - hp[:, cs]
        y = _dot(pooled.astype(BF16), poolw_ref[g].astype(BF16)) * pools_ref[:, cs]
        mix_ref[:, GW + g * LANES:GW + (g + 1) * LANES] = y.astype(BF16)

    glu_ext = jnp.concatenate([jnp.where(first, 0.0, halo_ref[1] * jax.nn.sigmoid(halo_ref[2])),
                               h_ref[4] * jax.nn.sigmoid(h_ref[5])], axis=0)
    cbuf_ref[0] = glu_ext
    for r in range(1, SUBLANES):
        cbuf_ref[r] = pltpu.roll(glu_ext, ts + HALO - r, 0)
    rchunk = 64
    for g in range(GW // LANES):
        cs = slice(g * LANES, (g + 1) * LANES)
        parts = []
        for r0 in range(0, ts, rchunk):
            acc = jnp.zeros((rchunk, LANES), F32)
            for j in range(CONV_WIDTH):
                lead = HALO - (CONV_WIDTH - 1) + j
                start = r0 + lead - lead % SUBLANES
                acc = acc + convw_ref[j:j + 1, cs] * cbuf_ref[lead % SUBLANES, start:start + rchunk, cs]
            parts.append(acc)
        y = jnp.concatenate(parts, axis=0) + convb_ref[:, cs]
        y = _ln(y, cvg_ref[:, cs], cvb_ref[:, cs])
        mix_ref[:, 2 * GW + g * LANES:2 * GW + (g + 1) * LANES] = jax.nn.silu(y).astype(BF16)

    q = h_ref[2]
    for hd in range(HEADS):
        cs = slice(hd * HEAD_DIM, (hd + 1) * HEAD_DIM)
        kh = kv_ref[:, hd * HEAD_DIM:(hd + 1) * HEAD_DIM]
        vh = kv_ref[:, GW + hd * HEAD_DIM:GW + (hd + 1) * HEAD_DIM]
        sc = _dot_nt(q[:, cs].astype(BF16), kh) * (HEAD_DIM ** -0.5)
        e = jnp.exp(sc - jnp.max(sc, axis=-1, keepdims=True))
        pr = e * (1.0 / jnp.sum(e, axis=-1, keepdims=True))
        mix_ref[:, 3 * GW + hd * HEAD_DIM:3 * GW + (hd + 1) * HEAD_DIM] = _dot(pr.astype(BF16), vh).astype(BF16)

    z = ALPHA * x_ref[...] + _dot(mix_ref[...], wout_ref[...])
    x1 = _ln(z, ln1g_ref[...], ln1b_ref[...])
    x1_ref[...] = x1
    x1bf_ref[...] = x1.astype(BF16)
    _pack_tokens(x1, xpk_ref)


def _mixers(h6, kv, x, wout_bf, p, l, batch, seq):
    n = x.shape[0]
    ns = seq // TS_MIX
    hb = TS_MIX // HALO

    def vec(width):
        return pl.BlockSpec((None, 1, width), lambda b, i: (l, 0, 0))

    return pl.pallas_call(
        _mix_kernel,
        grid=(batch, ns),
        in_specs=[
            pl.BlockSpec((6, TS_MIX, GW), lambda b, i: (0, b * ns + i, 0)),
            pl.BlockSpec((3, HALO, GW), lambda b, i: (1, jnp.maximum((b * ns + i) * hb - 1, 0), 0)),
            pl.BlockSpec((None, N_MEM, 2 * GW), lambda b, i: (l, b, 0)),
            pl.BlockSpec((TS_MIX, D_MODEL), lambda b, i: (b * ns + i, 0)),
            pl.BlockSpec((None, D_MODEL, D_MODEL), lambda b, i: (l, 0, 0)),
            pl.BlockSpec((None, HEADS, CHUNK, CHUNK), lambda b, i: (l, 0, 0, 0)),
            pl.BlockSpec((None, CHUNK, HEADS), lambda b, i: (l, 0, 0)),
            vec(GW), vec(GW),
            pl.BlockSpec((None, len(POOL_WINDOWS), LANES, LANES), lambda b, i: (l, 0, 0, 0)),
            vec(GW),
            pl.BlockSpec((None, CONV_WIDTH, GW), lambda b, i: (l, 0, 0)),
            vec(GW), vec(GW), vec(GW), vec(D_MODEL), vec(D_MODEL),
        ],
        out_specs=[
            pl.BlockSpec((TS_MIX, D_MODEL), lambda b, i: (b * ns + i, 0)),
            pl.BlockSpec((TS_MIX, D_MODEL), lambda b, i: (b * ns + i, 0)),
            pl.BlockSpec((TS_MIX * SUBLANES, LANES), lambda b, i: (b * ns + i, 0)),
        ],
        out_shape=[
            jax.ShapeDtypeStruct((n, D_MODEL), F32),
            jax.ShapeDtypeStruct((n, D_MODEL), BF16),
            jax.ShapeDtypeStruct((n * SUBLANES, LANES), U32),
        ],
        scratch_shapes=[
            pltpu.VMEM((TS_MIX, D_MODEL), BF16),
            pltpu.VMEM((SUBLANES, TS_MIX + HALO, GW), F32),
        ],
        compiler_params=pltpu.CompilerParams(vmem_limit_bytes=VMEM_LIMIT),
        name="mixers_outproj_ln1",
    )(h6, h6, kv, x, wout_bf, p['sgu_w'], p['sgu_b'], p['sgu_ln_g'], p['sgu_ln_b'],
      p['pool_w'], p['pool_scale'], p['conv_w'], p['conv_b'], p['conv_ln_g'], p['conv_ln_b'],
      p['ln1_g'], p['ln1_b'])


def _split_bf16(a):
    hi = a.astype(BF16)
    lo = (a - hi.astype(F32)).astype(BF16)
    return hi, lo


def _router_kernel(x_ref, wrt_ref, bias_ref, dest_ref, wdense_ref, emeta_ref, carry_ref, idx_ref):
    t = x_ref.shape[0]
    neg = -jnp.inf
    step = pl.program_id(0)

    @pl.when(step == 0)
    def _():
        carry_ref[...] = jnp.zeros_like(carry_ref)

    xh, xl = _split_bf16(x_ref[...])
    wh, wl = _split_bf16(wrt_ref[...])
    logits = _dot_nt(wh, xh) + (_dot_nt(wh, xl) + _dot_nt(wl, xh))
    scores = jax.nn.sigmoid(logits)
    choice = scores + bias_ref[...]

    iota8 = lax.broadcasted_iota(I32, (GROUP_SIZE, t), 0)
    rows = []
    for g in range(N_GROUPS):
        cg = choice[g * GROUP_SIZE:(g + 1) * GROUP_SIZE]
        m1 = jnp.max(cg, axis=0, keepdims=True)
        i1 = jnp.min(jnp.where(cg == m1, iota8, GROUP_SIZE), axis=0, keepdims=True)
        m2 = jnp.max(jnp.where(iota8 == i1, neg, cg), axis=0, keepdims=True)
        rows.append(m1 + m2)
    gwork = jnp.concatenate(rows, axis=0)

    gsel = jnp.zeros((N_GROUPS, t), jnp.bool_)
    for _ in range(TOPK_GROUPS):
        m = jnp.max(gwork, axis=0, keepdims=True)
        ii = jnp.min(jnp.where(gwork == m, iota8, N_GROUPS), axis=0, keepdims=True)
        hit = iota8 == ii
        gsel = gsel | hit
        gwork = jnp.where(hit, neg, gwork)
    emask = jnp.concatenate(
        [jnp.broadcast_to(gsel[g:g + 1], (GROUP_SIZE, t)) for g in range(N_GROUPS)], axis=0)

    iota_e = lax.broadcasted_iota(I32, (N_EXPERTS, t), 0)
    work = jnp.where(emask, choice, neg)
    sel = jnp.zeros((N_EXPERTS, t), jnp.bool_)
    idx_rows, raw_rows = [], []
    for _ in range(TOP_K):
        m = jnp.max(work, axis=0, keepdims=True)
        ii = jnp.min(jnp.where(work == m, iota_e, N_EXPERTS), axis=0, keepdims=True)
        hit = iota_e == ii
        idx_rows.append(ii)
        raw_rows.append(jnp.sum(jnp.where(hit, scores, 0.0), axis=0, keepdims=True))
        sel = sel | hit
        work = jnp.where(hit, neg, work)
    wsum = raw_rows[0]
    for r in raw_rows[1:]:
        wsum = wsum + r
    scale = ROUTED_SCALE / (wsum + 1e-20)

    dense = jnp.zeros((N_EXPERTS, t), F32)
    for k in range(TOP_K):
        dense = jnp.where(iota_e == idx_rows[k], raw_rows[k] * scale, dense)
    wdense_ref[...] = jnp.concatenate([dense, jnp.zeros((LANES - N_EXPERTS, t), F32)], axis=0).T

    upper = (lax.broadcasted_iota(I32, (t, t), 0) < lax.broadcasted_iota(I32, (t, t), 1))
    sel_f = sel.astype(F32)
    prefix = _dot(sel_f.astype(BF16), upper.astype(F32).astype(BF16)) + carry_ref[...]
    pos_rows = [jnp.sum(jnp.where(iota_e == idx_rows[k], prefix, 0.0), axis=0, keepdims=True)
                for k in range(TOP_K)]
    dest_ref[step] = jnp.concatenate(pos_rows, axis=0).astype(I32)
    idx_ref[step] = jnp.concatenate(idx_rows, axis=0)
    carry = carry_ref[...] + jnp.sum(sel_f, axis=1, keepdims=True)
    carry_ref[...] = carry

    @pl.when(step == pl.num_programs(0) - 1)
    def _():
        cnt = carry.astype(I32)
        nblk = (cnt + (BM_EXP - 1)) >> BM_SHIFT
        lower = (lax.broadcasted_iota(I32, (N_EXPERTS, N_EXPERTS), 0)
                 >= lax.broadcasted_iota(I32, (N_EXPERTS, N_EXPERTS), 1))
        nblk_wide = jnp.broadcast_to(nblk.astype(F32), (N_EXPERTS, LANES)).astype(BF16)
        bend = _dot(lower.astype(F32).astype(BF16), nblk_wide)[:, 0:1].astype(I32)
        bstart = bend - nblk
        pad_start = bstart << BM_SHIFT
        pad_rows = jnp.broadcast_to(pad_start, (N_EXPERTS, t))

        def add_start(tile, c):
            idx_t = idx_ref[tile]
            add = jnp.zeros((TOP_K, t), I32)
            for ex in range(N_EXPERTS):
                add = jnp.where(idx_t == ex, pad_rows[ex:ex + 1], add)
            dest_ref[tile] = dest_ref[tile] + add
            return c

        lax.fori_loop(0, dest_ref.shape[0], add_start, 0)

        eye = (lax.broadcasted_iota(I32, (N_EXPERTS, LANES), 0) == lax.broadcasted_iota(I32, (N_EXPERTS, LANES), 1))

        def as_row(col):
            return jnp.sum(jnp.where(eye, col, 0), axis=0, keepdims=True)

        emeta_ref[...] = jnp.concatenate(
            [as_row(pad_start + cnt), as_row(bend << BM_SHIFT), as_row(bstart), as_row(nblk),
             jnp.zeros((SUBLANES - 4, LANES), I32)], axis=0)


def _router(x1, w_router_t, bias, l):
    n = x1.shape[0]
    nt = n // T_ROUTE
    return pl.pallas_call(
        _router_kernel,
        grid=(nt,),
        in_specs=[
            pl.BlockSpec((T_ROUTE, D_MODEL), lambda t: (t, 0)),
            pl.BlockSpec((None, N_EXPERTS, D_MODEL), lambda t: (l, 0, 0)),
            pl.BlockSpec((None, N_EXPERTS, 1), lambda t: (l, 0, 0)),
        ],
        out_specs=[
            pl.BlockSpec((nt, TOP_K, T_ROUTE), lambda t: (0, 0, 0)),
            pl.BlockSpec((T_ROUTE, LANES), lambda t: (t, 0)),
            pl.BlockSpec((SUBLANES, LANES), lambda t: (0, 0)),
        ],
        out_shape=[
            jax.ShapeDtypeStruct((nt, TOP_K, T_ROUTE), I32),
            jax.ShapeDtypeStruct((n, LANES), F32),
            jax.ShapeDtypeStruct((SUBLANES, LANES), I32),
        ],
        scratch_shapes=[pltpu.VMEM((N_EXPERTS, 1), F32), pltpu.VMEM((nt, TOP_K, T_ROUTE), I32)],
        compiler_params=pltpu.CompilerParams(vmem_limit_bytes=VMEM_LIMIT),
        name="router",
    )(x1, w_router_t, bias)


def _slot_kernel(dest_ref, emeta_ref, off_ref, *, n_tok):
    spare = n_tok * SUBLANES
    group = SUBLANES

    def fill_range(start, end):
        def fill(j, c):
            for u in range(group):
                off_ref[start + j * group + u] = spare
            return c
        lax.fori_loop(0, (end - start + group - 1) // group, fill, 0)

    def fill_expert(ex, c):
        fill_range(emeta_ref[0, ex], emeta_ref[1, ex])
        return c

    lax.fori_loop(0, N_EXPERTS, fill_expert, 0)
    fill_range(emeta_ref[1, N_EXPERTS - 1], off_ref.shape[0])

    def tile_body(tile, c):
        for k in range(TOP_K):
            base = (tile * TOP_K + k) * T_ROUTE

            def inner(j, c2, base=base):
                src = base + j * SLOT_UNROLL
                val = (tile * T_ROUTE + j * SLOT_UNROLL) * SUBLANES
                for u in range(SLOT_UNROLL):
                    off_ref[dest_ref[src + u]] = val + u * SUBLANES
                return c2

            lax.fori_loop(0, T_ROUTE // SLOT_UNROLL, inner, 0)
        return c

    lax.fori_loop(0, n_tok // T_ROUTE, tile_body, 0)


def _slot_table(dest_flat, emeta, n, nb):
    smem = pl.BlockSpec(memory_space=pltpu.SMEM)
    return pl.pallas_call(
        functools.partial(_slot_kernel, n_tok=n),
        in_specs=[smem, smem],
        out_specs=smem,
        out_shape=jax.ShapeDtypeStruct(((nb + 1) * BM_EXP,), I32),
        name="slot_table",
    )(dest_flat, emeta)


def _block_rows(first_block, j):
    return pl.ds(pl.multiple_of((first_block + j) * BM_EXP, BM_EXP), BM_EXP)


def _gather_rows(g, off_ref, xpk_ref, wdense_ref, stage_ref, wrow_ref):
    row_mask = xpk_ref.shape[0] - 1
    base = g * BM_EXP
    for r in range(BM_EXP):
        off = off_ref[base + r] & row_mask
        stage_ref[r * SUBLANES:(r + 1) * SUBLANES, :] = xpk_ref[pl.ds(pl.multiple_of(off, SUBLANES), SUBLANES), :]
        wrow_ref[r:r + 1, :] = wdense_ref[pl.ds(off >> 3, 1), :]


def _e1_kernel(bstart_ref, nblk_ref, off_ref, xpk_ref, wdense_ref, wg_ref, wu_ref, h_hbm,
               stage_a, stage_b, wrow_a, wrow_b, lhs_ref, wcat_ref, hbuf_ref, sem):
    e = pl.program_id(0)
    nblk = nblk_ref[e]
    first = bstart_ref[e]
    stages = ((stage_a, wrow_a), (stage_b, wrow_b))

    def h_copy(g):
        return pltpu.make_async_copy(hbuf_ref.at[g & 1], h_hbm.at[_block_rows(0, g)], sem.at[g & 1])

    @pl.when(e == 0)
    def _():
        _gather_rows(0, off_ref, xpk_ref, wdense_ref, stage_a, wrow_a)

    @pl.when(nblk > 0)
    def _():
        wcat_ref[:, :D_EXPERT] = wg_ref[...].astype(BF16)
        wcat_ref[:, D_EXPERT:] = wu_ref[...].astype(BF16)
        lane = lax.broadcasted_iota(I32, (BM_EXP, LANES), 1)

        def compute(g, parity):
            stage_ref, wrow_ref = stages[parity]
            wcol = jnp.sum(jnp.where(lane == e, wrow_ref[...], 0.0), axis=1, keepdims=True)
            for s in range(SUBLANES):
                lo, hi = _unpack_tokens(stage_ref[pl.ds(s, BM_EXP, stride=SUBLANES), :])
                lhs_ref[:, s * LANES:(s + 1) * LANES] = lo
                lhs_ref[:, HALF + s * LANES:HALF + (s + 1) * LANES] = hi
            gu = _dot(lhs_ref[...], wcat_ref[...])
            hbuf_ref[parity] = (jax.nn.silu(gu[:, :D_EXPERT]) * gu[:, D_EXPERT:] * wcol).astype(BF16)
            h_copy(g).start()

        def block(j, carry):
            g = first + j

            @pl.when(g >= 2)
            def _():
                h_copy(g - 2).wait()

            for parity in (0, 1):
                @pl.when((g & 1) == parity)
                def _(parity=parity):
                    _gather_rows(g + 1, off_ref, xpk_ref, wdense_ref, *stages[1 - parity])
                    compute(g, parity)
            return carry

        lax.fori_loop(0, nblk, block, 0)

    @pl.when(e == N_EXPERTS - 1)
    def _():
        total = first + nblk

        @pl.when(total >= 2)
        def _():
            h_copy(total - 2).wait()

        @pl.when(total >= 1)
        def _():
            h_copy(total - 1).wait()

        hbuf_ref[0] = jnp.zeros((BM_EXP, D_EXPERT), BF16)

        def zero_block(j, carry):
            copy = pltpu.make_async_copy(hbuf_ref.at[0], h_hbm.at[_block_rows(0, j)], sem.at[0])
            copy.start()
            copy.wait()
            return carry

        lax.fori_loop(first + nblk, h_hbm.shape[0] // BM_EXP, zero_block, 0)


def _expert_gate_up(bstart, nblk, slot_off, xpk, wdense, exp_w_gate, exp_w_up, l):
    assert xpk.shape[0] & (xpk.shape[0] - 1) == 0
    grid_spec = pltpu.PrefetchScalarGridSpec(
        num_scalar_prefetch=3,
        grid=(N_EXPERTS,),
        in_specs=[
            pl.BlockSpec(xpk.shape, lambda e, *_: (0, 0), pipeline_mode=pl.Buffered(1)),
            pl.BlockSpec(wdense.shape, lambda e, *_: (0, 0), pipeline_mode=pl.Buffered(1)),
            pl.BlockSpec((None, None, D_MODEL, D_EXPERT), lambda e, *_: (l, e, 0, 0)),
            pl.BlockSpec((None, None, D_MODEL, D_EXPERT), lambda e, *_: (l, e, 0, 0)),
        ],
        out_specs=pl.BlockSpec(memory_space=pl.ANY),
        scratch_shapes=[
            pltpu.VMEM((BM_EXP * SUBLANES, LANES), U32),
            pltpu.VMEM((BM_EXP * SUBLANES, LANES), U32),
            pltpu.VMEM((BM_EXP, LANES), F32),
            pltpu.VMEM((BM_EXP, LANES), F32),
            pltpu.VMEM((BM_EXP, D_MODEL), BF16),
            pltpu.VMEM((D_MODEL, 2 * D_EXPERT), BF16),
            pltpu.VMEM((2, BM_EXP, D_EXPERT), BF16),
            pltpu.SemaphoreType.DMA((2,)),
        ],
    )
    return pl.pallas_call(
        _e1_kernel,
        grid_spec=grid_spec,
        out_shape=jax.ShapeDtypeStruct((slot_off.shape[0], D_EXPERT), BF16),
        compiler_params=pltpu.CompilerParams(vmem_limit_bytes=VMEM_LIMIT),
        name="expert_gate_up",
    )(bstart, nblk, slot_off, xpk, wdense, exp_w_gate, exp_w_up)


def _e2_kernel(bstart_ref, nblk_ref, off_ref, h_hbm, wd_ref, out_ref, ybuf_a, ybuf_b, hbuf_ref, wdb_ref, sem):
    e = pl.program_id(1)
    nblk = nblk_ref[e]
    first = bstart_ref[e]
    total = bstart_ref[N_EXPERTS - 1] + nblk_ref[N_EXPERTS - 1]
    group = SUBLANES
    chunks = HALF // LANES
    ybufs = (ybuf_a, ybuf_b)

    def h_copy(g):
        slot = g % H_SLOTS
        return pltpu.make_async_copy(h_hbm.at[_block_rows(0, g)], hbuf_ref.at[slot], sem.at[slot])

    def project(g, ybuf_ref):
        y = _dot(hbuf_ref[g % H_SLOTS], wdb_ref[...])
        for t in range(BM_EXP // SUBLANES):
            for c in range(chunks):
                t0 = (t * chunks + c) * SUBLANES
                ybuf_ref[t0:t0 + SUBLANES, :] = y[t * SUBLANES:(t + 1) * SUBLANES, c * LANES:(c + 1) * LANES]

    def scatter(g, ybuf_ref):
        base = g * BM_EXP
        for g0 in range(0, BM_EXP, group):
            dsts, vals = [], []
            for r in range(g0, g0 + group):
                dst = pl.ds(pl.multiple_of(off_ref[base + r], SUBLANES), SUBLANES)
                row = ybuf_ref[pl.ds((r // SUBLANES) * chunks * SUBLANES + r % SUBLANES, chunks, stride=SUBLANES), :]
                dsts.append(dst)
                vals.append(out_ref[dst, :] + row)
            for dst, val in zip(dsts, vals):
                out_ref[dst, :] = val

    @pl.when(e == 0)
    def _():
        out_ref[...] = jnp.zeros_like(out_ref)
        for g in range(H_SLOTS):
            @pl.when(g <= total)
            def _(g=g):
                h_copy(g).start()

    @pl.when(nblk > 0)
    def _():
        wdb_ref[...] = wd_ref[...].astype(BF16)

        @pl.when(first == 0)
        def _():
            h_copy(0).wait()

        for parity in (0, 1):
            @pl.when((first & 1) == parity)
            def _(parity=parity):
                project(first, ybufs[parity])

        def block(j, carry):
            g = first + j
            h_copy(g + 1).wait()

            for parity in (0, 1):
                @pl.when((g & 1) == parity)
                def _(parity=parity):
                    project(g + 1, ybufs[1 - parity])
                    scatter(g, ybufs[parity])

            @pl.when(g + H_SLOTS <= total)
            def _():
                h_copy(g + H_SLOTS).start()
            return carry

        lax.fori_loop(0, nblk, block, 0)


def _expert_down_combine(bstart, nblk, slot_off, hmid, exp_w_down, l, n):
    rows = (n + 1) * SUBLANES
    grid_spec = pltpu.PrefetchScalarGridSpec(
        num_scalar_prefetch=3,
        grid=(D_MODEL // HALF, N_EXPERTS),
        in_specs=[
            pl.BlockSpec(memory_space=pl.ANY),
            pl.BlockSpec((None, None, D_EXPERT, HALF), lambda c, e, *_: (l, e, 0, c)),
        ],
        out_specs=pl.BlockSpec((None, rows, LANES), lambda c, e, *_: (c, 0, 0),
                               pipeline_mode=pl.Buffered(1)),
        scratch_shapes=[
            pltpu.VMEM((BM_EXP * HALF // LANES, LANES), F32),
            pltpu.VMEM((BM_EXP * HALF // LANES, LANES), F32),
            pltpu.VMEM((H_SLOTS, BM_EXP, D_EXPERT), BF16),
            pltpu.VMEM((D_EXPERT, HALF), BF16),
            pltpu.SemaphoreType.DMA((H_SLOTS,)),
        ],
    )
    return pl.pallas_call(
        _e2_kernel,
        grid_spec=grid_spec,
        out_shape=jax.ShapeDtypeStruct((D_MODEL // HALF, rows, LANES), F32),
        compiler_params=pltpu.CompilerParams(vmem_limit_bytes=VMEM_LIMIT),
        name="expert_down_combine",
    )(bstart, nblk, slot_off, hmid, exp_w_down)


def _final_kernel(x1_ref, x1bf_ref, routed_ref, wgu_ref, wd_ref, g_ref, b_ref, x2_ref, x2bf_ref):
    ts = x1_ref.shape[0]
    gu = _dot(x1bf_ref[...], wgu_ref[...])
    hs = (jax.nn.silu(gu[:, :D_SHARED]) * gu[:, D_SHARED:]).astype(BF16)
    shared = _dot(hs, wd_ref[...])
    routed = jnp.concatenate(
        [routed_ref[c, pl.ds(s, ts, stride=SUBLANES), :]
         for c in range(D_MODEL // HALF) for s in range(SUBLANES)], axis=1)
    z = ALPHA * x1_ref[...] + (shared + routed)
    x2 = _ln(z, g_ref[...], b_ref[...])
    x2_ref[...] = x2
    x2bf_ref[...] = x2.astype(BF16)


def _shared_ln2(x1, x1bf, routed, sh_gu_bf, sh_d_bf, ln2_g, ln2_b, l):
    n = x1.shape[0]
    row_spec = pl.BlockSpec((TS_FIN, D_MODEL), lambda i: (i, 0))
    return pl.pallas_call(
        _final_kernel,
        grid=(n // TS_FIN,),
        in_specs=[
            row_spec, row_spec,
            pl.BlockSpec((D_MODEL // HALF, TS_FIN * SUBLANES, LANES), lambda i: (0, i, 0)),
            pl.BlockSpec((None, D_MODEL, 2 * D_SHARED), lambda i: (l, 0, 0)),
            pl.BlockSpec((None, D_SHARED, D_MODEL), lambda i: (l, 0, 0)),
            pl.BlockSpec((None, 1, D_MODEL), lambda i: (l, 0, 0)),
            pl.BlockSpec((None, 1, D_MODEL), lambda i: (l, 0, 0)),
        ],
        out_specs=[row_spec, row_spec],
        out_shape=[jax.ShapeDtypeStruct((n, D_MODEL), F32), jax.ShapeDtypeStruct((n, D_MODEL), BF16)],
        compiler_params=pltpu.CompilerParams(vmem_limit_bytes=VMEM_LIMIT),
        name="shared_ln2",
    )(x1, x1bf, routed, sh_gu_bf, sh_d_bf, ln2_g, ln2_b)


def kernel(x, mem, mem_ln_g, mem_ln_b, w_in, sgu_ln_g, sgu_ln_b, sgu_w, sgu_b, pool_w, pool_scale, conv_w, conv_b, conv_ln_g, conv_ln_b, w_mem_kv, w_out, ln1_g, ln1_b, w_router, router_bias, exp_w_gate, exp_w_up, exp_w_down, sh_w_gate, sh_w_up, sh_w_down, ln2_g, ln2_b):
    batch, seq, d = x.shape
    n = batch * seq
    depth = w_in.shape[0]
    assert (d, depth) == (D_MODEL, DEPTH) and seq % TS_MIX == 0 and n % BM_IN == 0 and n % T_ROUTE == 0

    w_in_p = jnp.concatenate([w_in[..., :2 * GW], w_in[..., 5 * GW:], w_in[..., 2 * GW:5 * GW]], axis=-1).astype(BF16)
    wout_bf = w_out.astype(BF16)
    sh_gu_bf = jnp.concatenate([sh_w_gate, sh_w_up], axis=-1).astype(BF16)
    sh_d_bf = sh_w_down.astype(BF16)
    w_router_t = jnp.swapaxes(w_router, 1, 2)
    bias3 = router_bias[..., None]

    def row(a):
        return a[:, None, :]

    p = dict(sgu_w=sgu_w, sgu_b=sgu_b, sgu_ln_g=row(sgu_ln_g), sgu_ln_b=row(sgu_ln_b), pool_w=pool_w,
             pool_scale=row(pool_scale), conv_w=conv_w, conv_b=row(conv_b), conv_ln_g=row(conv_ln_g),
             conv_ln_b=row(conv_ln_b), ln1_g=row(ln1_g), ln1_b=row(ln1_b))
    ln2_g3, ln2_b3 = row(ln2_g), row(ln2_b)

    kv = _memory_kv(mem.reshape(batch * N_MEM, d), mem_ln_g[None, :], mem_ln_b[None, :], w_mem_kv)

    nb = n * TOP_K // BM_EXP + N_EXPERTS
    assert (1 << BM_SHIFT) == BM_EXP

    xf = x.reshape(n, d)
    xbf = xf.astype(BF16)
    for l in range(depth):
        h6 = _in_proj(xbf, w_in_p, l)
        x1, x1bf, xpk = _mixers(h6, kv, xf, wout_bf, p, l, batch, seq)
        dest, wdense, emeta = _router(x1, w_router_t, bias3, l)
        slot_off = _slot_table(dest.reshape(-1), emeta, n, nb)
        bstart, nblk = emeta[2, :N_EXPERTS], emeta[3, :N_EXPERTS]
        hmid = _expert_gate_up(bstart, nblk, slot_off, xpk, wdense, exp_w_gate, exp_w_up, l)
        routed = _expert_down_combine(bstart, nblk, slot_off, hmid, exp_w_down, l, n)
        xf, xbf = _shared_ln2(x1, x1bf, routed, sh_gu_bf, sh_d_bf, ln2_g3, ln2_b3, l)
    return xf.reshape(batch, seq, d)
```

```python
import functools

import jax
import jax.numpy as jnp
from jax import lax
from jax.experimental import pallas as pl
from jax.experimental.pallas import tpu as pltpu

D_MODEL = 2048
DEPTH = 4
GW = 512
HEAD_DIM = 128
HEADS = 4
CHUNK = 128
POOL_WINDOWS = (2, 4, 8, 16)
CONV_WIDTH = 31
N_MEM = 256
N_EXPERTS = 64
N_GROUPS = 8
GROUP_SIZE = N_EXPERTS // N_GROUPS
TOPK_GROUPS = 4
TOP_K = 8
D_EXPERT = 256
D_SHARED = 256
ROUTED_SCALE = 2.5
LN_EPS = 1e-5
ALPHA = (2.0 * DEPTH) ** 0.25

LANES = 128
SUBLANES = 8
HALF = D_MODEL // 2
HALO = 32
VMEM_LIMIT = 56 * 1024 * 1024

TS_MIX = 256
BM_IN = 512
BN_IN = 1024
T_ROUTE = 512
BM_EXP = 256
BM_SHIFT = 8
TS_FIN = 256
SLOT_UNROLL = 16
H_SLOTS = 4

F32 = jnp.float32
BF16 = jnp.bfloat16
U32 = jnp.uint32
I32 = jnp.int32


def _ln(x, g, b):
    mu = jnp.mean(x, axis=-1, keepdims=True)
    xc = x - mu
    var = jnp.mean(xc * xc, axis=-1, keepdims=True)
    return xc * lax.rsqrt(var + LN_EPS) * g + b


def _dot(a, b):
    return jnp.dot(a, b, preferred_element_type=F32)


def _dot_nt(a, b):
    return lax.dot_general(a, b, (((1,), (1,)), ((), ())), preferred_element_type=F32)


def _kv_kernel(mem_ref, g_ref, b_ref, w_ref, o_ref):
    memn = _ln(mem_ref[...], g_ref[...], b_ref[...])
    o_ref[...] = _dot(memn.astype(BF16), w_ref[...].astype(BF16)).astype(BF16)


def _memory_kv(mem2, g, b, w_mem_kv):
    nb = mem2.shape[0] // N_MEM
    return pl.pallas_call(
        _kv_kernel,
        grid=(DEPTH, nb),
        in_specs=[
            pl.BlockSpec((N_MEM, D_MODEL), lambda l, i: (i, 0)),
            pl.BlockSpec((1, D_MODEL), lambda l, i: (0, 0)),
            pl.BlockSpec((1, D_MODEL), lambda l, i: (0, 0)),
            pl.BlockSpec((None, D_MODEL, 2 * GW), lambda l, i: (l, 0, 0)),
        ],
        out_specs=pl.BlockSpec((None, N_MEM, 2 * GW), lambda l, i: (l, i, 0)),
        out_shape=jax.ShapeDtypeStruct((DEPTH, mem2.shape[0], 2 * GW), BF16),
        compiler_params=pltpu.CompilerParams(vmem_limit_bytes=VMEM_LIMIT),
        name="memory_kv",
    )(mem2, g, b, w_mem_kv)


def _inproj_kernel(x_ref, w_ref, o_ref):
    acc = _dot(x_ref[...], w_ref[...])
    o_ref[0] = acc[:, :GW]
    o_ref[1] = acc[:, GW:]


def _in_proj(xbf, w_in_p, l):
    n = xbf.shape[0]
    return pl.pallas_call(
        _inproj_kernel,
        grid=(6 * GW // BN_IN, n // BM_IN),
        in_specs=[
            pl.BlockSpec((BM_IN, D_MODEL), lambda j, i: (i, 0)),
            pl.BlockSpec((None, D_MODEL, BN_IN), lambda j, i: (l, 0, j)),
        ],
        out_specs=pl.BlockSpec((BN_IN // GW, BM_IN, GW), lambda j, i: (j, i, 0)),
        out_shape=jax.ShapeDtypeStruct((6, n, GW), F32),
        compiler_params=pltpu.CompilerParams(vmem_limit_bytes=VMEM_LIMIT),
        name="in_proj",
    )(xbf, w_in_p)


def _pack_tokens(x, xpk_ref):
    rows = x.shape[0]
    for s in range(SUBLANES):
        lo = x[:, s * LANES:(s + 1) * LANES]
        hi = x[:, HALF + s * LANES:HALF + (s + 1) * LANES]
        xpk_ref[pl.ds(s, rows, stride=SUBLANES), :] = pltpu.pack_elementwise([lo, hi], packed_dtype=BF16)


def _unpack_tokens(word):
    lo = pltpu.unpack_elementwise(word, index=0, packed_dtype=BF16, unpacked_dtype=F32)
    hi = pltpu.unpack_elementwise(word, index=1, packed_dtype=BF16, unpacked_dtype=F32)
    return lo.astype(BF16), hi.astype(BF16)


def _mix_kernel(h_ref, halo_ref, kv_ref, x_ref, wout_ref, sguw_ref, sgub_ref, sgug_ref, sgubeta_ref,
                poolw_ref, pools_ref, convw_ref, convb_ref, cvg_ref, cvb_ref, ln1g_ref, ln1b_ref,
                x1_ref, xpk_ref, mix_ref, cbuf_ref):
    i = pl.program_id(1)
    ts = h_ref.shape[1]
    first = i == 0

    u = jax.nn.gelu(h_ref[0])
    v = jax.nn.gelu(h_ref[1])
    tri = (lax.broadcasted_iota(I32, (CHUNK, CHUNK), 0) >= lax.broadcasted_iota(I32, (CHUNK, CHUNK), 1))
    for hd in range(HEADS):
        cs = slice(hd * HEAD_DIM, (hd + 1) * HEAD_DIM)
        vn = _ln(v[:, cs], sgug_ref[:, cs], sgubeta_ref[:, cs]).astype(BF16)
        wm = jnp.where(tri, sguw_ref[hd], 0.0).astype(BF16)
        bcol = sgub_ref[:, hd:hd + 1]
        for c in range(ts // CHUNK):
            rs = slice(c * CHUNK, (c + 1) * CHUNK)
            mixed = _dot(wm, vn[rs]) + bcol
            mix_ref[rs, cs] = (u[rs, cs] * mixed).astype(BF16)

    hp = h_ref[3]
    halo_p = jnp.where(first, 0.0, halo_ref[0])
    ext = jnp.concatenate([halo_p, hp], axis=0)
    pos1 = (i * ts + lax.broadcasted_iota(I32, (ts, 1), 0) + 1).astype(F32)
    for g, win in enumerate(POOL_WINDOWS):
        cs = slice(g * LANES, (g + 1) * LANES)
        s = ext[:, cs]
        sh = 1
        while sh < win:
            s = s + pltpu.roll(s, sh, 0)
            sh *= 2
        cnt = jnp.minimum(pos1, float(win))
        pooled = s[HALO:] / cnt - hp[:, cs]
        y = _dot(pooled.astype(BF16), poolw_ref[g].astype(BF16)) * pools_ref[:, cs]
        mix_ref[:, GW + g * LANES:GW + (g + 1) * LANES] = y.astype(BF16)

    glu_ext = jnp.concatenate([jnp.where(first, 0.0, halo_ref[1] * jax.nn.sigmoid(halo_ref[2])),
                               h_ref[4] * jax.nn.sigmoid(h_ref[5])], axis=0)
    cbuf_ref[0] = glu_ext
    for r in range(1, SUBLANES):
        cbuf_ref[r] = pltpu.roll(glu_ext, ts + HALO - r, 0)
    rchunk = 64
    for g in range(GW // LANES):
        cs = slice(g * LANES, (g + 1) * LANES)
        parts = []
        for r0 in range(0, ts, rchunk):
            acc = jnp.zeros((rchunk, LANES), F32)
            for j in range(CONV_WIDTH):
                lead = HALO - (CONV_WIDTH - 1) + j
                start = r0 + lead - lead % SUBLANES
                acc = acc + convw_ref[j:j + 1, cs] * cbuf_ref[lead % SUBLANES, start:start + rchunk, cs]
            parts.append(acc)
        y = jnp.concatenate(parts, axis=0) + convb_ref[:, cs]
        y = _ln(y, cvg_ref[:, cs], cvb_ref[:, cs])
        mix_ref[:, 2 * GW + g * LANES:2 * GW + (g + 1) * LANES] = jax.nn.silu(y).astype(BF16)

    q = h_ref[2]
    for hd in range(HEADS):
        cs = slice(hd * HEAD_DIM, (hd + 1) * HEAD_DIM)
        kh = kv_ref[:, hd * HEAD_DIM:(hd + 1) * HEAD_DIM]
        vh = kv_ref[:, GW + hd * HEAD_DIM:GW + (hd + 1) * HEAD_DIM]
        sc = _dot_nt(q[:, cs].astype(BF16), kh) * (HEAD_DIM ** -0.5)
        e = jnp.exp(sc - jnp.max(sc, axis=-1, keepdims=True))
        pr = e * (1.0 / jnp.sum(e, axis=-1, keepdims=True))
        mix_ref[:, 3 * GW + hd * HEAD_DIM:3 * GW + (hd + 1) * HEAD_DIM] = _dot(pr.astype(BF16), vh).astype(BF16)

    z = ALPHA * x_ref[...] + _dot(mix_ref[...], wout_ref[...])
    x1 = _ln(z, ln1g_ref[...], ln1b_ref[...])
    x1_ref[...] = x1
    _pack_tokens(x1, xpk_ref)


def _mixers(h6, kv, x, wout_bf, p, l, batch, seq):
    n = x.shape[0]
    ns = seq // TS_MIX
    hb = TS_MIX // HALO

    def vec(width):
        return pl.BlockSpec((None, 1, width), lambda b, i: (l, 0, 0))

    return pl.pallas_call(
        _mix_kernel,
        grid=(batch, ns),
        in_specs=[
            pl.BlockSpec((6, TS_MIX, GW), lambda b, i: (0, b * ns + i, 0)),
            pl.BlockSpec((3, HALO, GW), lambda b, i: (1, jnp.maximum((b * ns + i) * hb - 1, 0), 0)),
            pl.BlockSpec((None, N_MEM, 2 * GW), lambda b, i: (l, b, 0)),
            pl.BlockSpec((TS_MIX, D_MODEL), lambda b, i: (b * ns + i, 0)),
            pl.BlockSpec((None, D_MODEL, D_MODEL), lambda b, i: (l, 0, 0)),
            pl.BlockSpec((None, HEADS, CHUNK, CHUNK), lambda b, i: (l, 0, 0, 0)),
            pl.BlockSpec((None, CHUNK, HEADS), lambda b, i: (l, 0, 0)),
            vec(GW), vec(GW),
            pl.BlockSpec((None, len(POOL_WINDOWS), LANES, LANES), lambda b, i: (l, 0, 0, 0)),
            vec(GW),
            pl.BlockSpec((None, CONV_WIDTH, GW), lambda b, i: (l, 0, 0)),
            vec(GW), vec(GW), vec(GW), vec(D_MODEL), vec(D_MODEL),
        ],
        out_specs=[
            pl.BlockSpec((TS_MIX, D_MODEL), lambda b, i: (b * ns + i, 0)),
            pl.BlockSpec((TS_MIX * SUBLANES, LANES), lambda b, i: (b * ns + i, 0)),
        ],
        out_shape=[
            jax.ShapeDtypeStruct((n, D_MODEL), F32),
            jax.ShapeDtypeStruct((n * SUBLANES, LANES), U32),
        ],
        scratch_shapes=[
            pltpu.VMEM((TS_MIX, D_MODEL), BF16),
            pltpu.VMEM((SUBLANES, TS_MIX + HALO, GW), F32),
        ],
        compiler_params=pltpu.CompilerParams(vmem_limit_bytes=VMEM_LIMIT),
        name="mixers_outproj_ln1",
    )(h6, h6, kv, x, wout_bf, p['sgu_w'], p['sgu_b'], p['sgu_ln_g'], p['sgu_ln_b'],
      p['pool_w'], p['pool_scale'], p['conv_w'], p['conv_b'], p['conv_ln_g'], p['conv_ln_b'],
      p['ln1_g'], p['ln1_b'])


def _split_bf16(a):
    hi = a.astype(BF16)
    lo = (a - hi.astype(F32)).astype(BF16)
    return hi, lo


def _router_kernel(x_ref, wrt_ref, bias_ref, dest_ref, wdense_ref, emeta_ref, carry_ref, idx_ref):
    t = x_ref.shape[0]
    neg = -jnp.inf
    step = pl.program_id(0)

    @pl.when(step == 0)
    def _():
        carry_ref[...] = jnp.zeros_like(carry_ref)

    xh, xl = _split_bf16(x_ref[...])
    wh, wl = _split_bf16(wrt_ref[...])
    logits = _dot_nt(wh, xh) + (_dot_nt(wh, xl) + _dot_nt(wl, xh))
    scores = jax.nn.sigmoid(logits)
    choice = scores + bias_ref[...]

    iota8 = lax.broadcasted_iota(I32, (GROUP_SIZE, t), 0)
    rows = []
    for g in range(N_GROUPS):
        cg = choice[g * GROUP_SIZE:(g + 1) * GROUP_SIZE]
        m1 = jnp.max(cg, axis=0, keepdims=True)
        i1 = jnp.min(jnp.where(cg == m1, iota8, GROUP_SIZE), axis=0, keepdims=True)
        m2 = jnp.max(jnp.where(iota8 == i1, neg, cg), axis=0, keepdims=True)
        rows.append(m1 + m2)
    gwork = jnp.concatenate(rows, axis=0)

    gsel = jnp.zeros((N_GROUPS, t), jnp.bool_)
    for _ in range(TOPK_GROUPS):
        m = jnp.max(gwork, axis=0, keepdims=True)
        ii = jnp.min(jnp.where(gwork == m, iota8, N_GROUPS), axis=0, keepdims=True)
        hit = iota8 == ii
        gsel = gsel | hit
        gwork = jnp.where(hit, neg, gwork)
    emask = jnp.concatenate(
        [jnp.broadcast_to(gsel[g:g + 1], (GROUP_SIZE, t)) for g in range(N_GROUPS)], axis=0)

    iota_e = lax.broadcasted_iota(I32, (N_EXPERTS, t), 0)
    work = jnp.where(emask, choice, neg)
    sel = jnp.zeros((N_EXPERTS, t), jnp.bool_)
    idx_rows, raw_rows = [], []
    for _ in range(TOP_K):
        m = jnp.max(work, axis=0, keepdims=True)
        ii = jnp.min(jnp.where(work == m, iota_e, N_EXPERTS), axis=0, keepdims=True)
        hit = iota_e == ii
        idx_rows.append(ii)
        raw_rows.append(jnp.sum(jnp.where(hit, scores, 0.0), axis=0, keepdims=True))
        sel = sel | hit
        work = jnp.where(hit, neg, work)
    wsum = raw_rows[0]
    for r in raw_rows[1:]:
        wsum = wsum + r
    scale = ROUTED_SCALE / (wsum + 1e-20)

    dense = jnp.zeros((N_EXPERTS, t), F32)
    for k in range(TOP_K):
        dense = jnp.where(iota_e == idx_rows[k], raw_rows[k] * scale, dense)
    wdense_ref[...] = jnp.concatenate([dense, jnp.zeros((LANES - N_EXPERTS, t), F32)], axis=0).T

    upper = (lax.broadcasted_iota(I32, (t, t), 0) < lax.broadcasted_iota(I32, (t, t), 1))
    sel_f = sel.astype(F32)
    prefix = _dot(sel_f.astype(BF16), upper.astype(F32).astype(BF16)) + carry_ref[...]
    pos_rows = [jnp.sum(jnp.where(iota_e == idx_rows[k], prefix, 0.0), axis=0, keepdims=True)
                for k in range(TOP_K)]
    dest_ref[step] = jnp.concatenate(pos_rows, axis=0).astype(I32)
    idx_ref[step] = jnp.concatenate(idx_rows, axis=0)
    carry = carry_ref[...] + jnp.sum(sel_f, axis=1, keepdims=True)
    carry_ref[...] = carry

    @pl.when(step == pl.num_programs(0) - 1)
    def _():
        cnt = carry.astype(I32)
        nblk = (cnt + (BM_EXP - 1)) >> BM_SHIFT
        lower = (lax.broadcasted_iota(I32, (N_EXPERTS, N_EXPERTS), 0)
                 >= lax.broadcasted_iota(I32, (N_EXPERTS, N_EXPERTS), 1))
        nblk_wide = jnp.broadcast_to(nblk.astype(F32), (N_EXPERTS, LANES)).astype(BF16)
        bend = _dot(lower.astype(F32).astype(BF16), nblk_wide)[:, 0:1].astype(I32)
        bstart = bend - nblk
        pad_start = bstart << BM_SHIFT
        pad_rows = jnp.broadcast_to(pad_start, (N_EXPERTS, t))

        def add_start(tile, c):
            idx_t = idx_ref[tile]
            add = jnp.zeros((TOP_K, t), I32)
            for ex in range(N_EXPERTS):
                add = jnp.where(idx_t == ex, pad_rows[ex:ex + 1], add)
            dest_ref[tile] = dest_ref[tile] + add
            return c

        lax.fori_loop(0, dest_ref.shape[0], add_start, 0)

        eye = (lax.broadcasted_iota(I32, (N_EXPERTS, LANES), 0) == lax.broadcasted_iota(I32, (N_EXPERTS, LANES), 1))

        def as_row(col):
            return jnp.sum(jnp.where(eye, col, 0), axis=0, keepdims=True)

        emeta_ref[...] = jnp.concatenate(
            [as_row(pad_start + cnt), as_row(bend << BM_SHIFT), as_row(bstart), as_row(nblk),
             jnp.zeros((SUBLANES - 4, LANES), I32)], axis=0)


def _router(x1, w_router_t, bias, l):
    n = x1.shape[0]
    nt = n // T_ROUTE
    return pl.pallas_call(
        _router_kernel,
        grid=(nt,),
        in_specs=[
            pl.BlockSpec((T_ROUTE, D_MODEL), lambda t: (t, 0)),
            pl.BlockSpec((None, N_EXPERTS, D_MODEL), lambda t: (l, 0, 0)),
            pl.BlockSpec((None, N_EXPERTS, 1), lambda t: (l, 0, 0)),
        ],
        out_specs=[
            pl.BlockSpec((nt, TOP_K, T_ROUTE), lambda t: (0, 0, 0)),
            pl.BlockSpec((T_ROUTE, LANES), lambda t: (t, 0)),
            pl.BlockSpec((SUBLANES, LANES), lambda t: (0, 0)),
        ],
        out_shape=[
            jax.ShapeDtypeStruct((nt, TOP_K, T_ROUTE), I32),
            jax.ShapeDtypeStruct((n, LANES), F32),
            jax.ShapeDtypeStruct((SUBLANES, LANES), I32),
        ],
        scratch_shapes=[pltpu.VMEM((N_EXPERTS, 1), F32), pltpu.VMEM((nt, TOP_K, T_ROUTE), I32)],
        compiler_params=pltpu.CompilerParams(vmem_limit_bytes=VMEM_LIMIT),
        name="router",
    )(x1, w_router_t, bias)


def _slot_kernel(dest_ref, emeta_ref, off_ref, *, n_tok):
    spare = n_tok * SUBLANES
    group = SUBLANES

    def fill_range(start, end):
        def fill(j, c):
            for u in range(group):
                off_ref[start + j * group + u] = spare
            return c
        lax.fori_loop(0, (end - start + group - 1) // group, fill, 0)

    def fill_expert(ex, c):
        fill_range(emeta_ref[0, ex], emeta_ref[1, ex])
        return c

    lax.fori_loop(0, N_EXPERTS, fill_expert, 0)
    fill_range(emeta_ref[1, N_EXPERTS - 1], off_ref.shape[0])

    def tile_body(tile, c):
        for k in range(TOP_K):
            base = (tile * TOP_K + k) * T_ROUTE

            def inner(j, c2, base=base):
                src = base + j * SLOT_UNROLL
                val = (tile * T_ROUTE + j * SLOT_UNROLL) * SUBLANES
                for u in range(SLOT_UNROLL):
                    off_ref[dest_ref[src + u]] = val + u * SUBLANES
                return c2

            lax.fori_loop(0, T_ROUTE // SLOT_UNROLL, inner, 0)
        return c

    lax.fori_loop(0, n_tok // T_ROUTE, tile_body, 0)


def _slot_table(dest_flat, emeta, n, nb):
    smem = pl.BlockSpec(memory_space=pltpu.SMEM)
    return pl.pallas_call(
        functools.partial(_slot_kernel, n_tok=n),
        in_specs=[smem, smem],
        out_specs=smem,
        out_shape=jax.ShapeDtypeStruct(((nb + 1) * BM_EXP,), I32),
        name="slot_table",
    )(dest_flat, emeta)


def _block_rows(first_block, j):
    return pl.ds(pl.multiple_of((first_block + j) * BM_EXP, BM_EXP), BM_EXP)


def _gather_rows(g, off_ref, xpk_ref, wdense_ref, stage_ref, wrow_ref):
    row_mask = xpk_ref.shape[0] - 1
    base = g * BM_EXP
    for r in range(BM_EXP):
        off = off_ref[base + r] & row_mask
        stage_ref[r * SUBLANES:(r + 1) * SUBLANES, :] = xpk_ref[pl.ds(pl.multiple_of(off, SUBLANES), SUBLANES), :]
        wrow_ref[r:r + 1, :] = wdense_ref[pl.ds(off >> 3, 1), :]


def _e1_kernel(bstart_ref, nblk_ref, off_ref, xpk_ref, wdense_ref, wg_ref, wu_ref, h_hbm,
               stage_a, stage_b, wrow_a, wrow_b, lhs_ref, wcat_ref, hbuf_ref, sem):
    e = pl.program_id(0)
    nblk = nblk_ref[e]
    first = bstart_ref[e]
    stages = ((stage_a, wrow_a), (stage_b, wrow_b))

    def h_copy(g):
        return pltpu.make_async_copy(hbuf_ref.at[g & 1], h_hbm.at[_block_rows(0, g)], sem.at[g & 1])

    @pl.when(e == 0)
    def _():
        _gather_rows(0, off_ref, xpk_ref, wdense_ref, stage_a, wrow_a)

    @pl.when(nblk > 0)
    def _():
        wcat_ref[:, :D_EXPERT] = wg_ref[...].astype(BF16)
        wcat_ref[:, D_EXPERT:] = wu_ref[...].astype(BF16)
        lane = lax.broadcasted_iota(I32, (BM_EXP, LANES), 1)

        def compute(g, parity):
            stage_ref, wrow_ref = stages[parity]
            next_stage, next_wrow = stages[1 - parity]
            wcol = jnp.sum(jnp.where(lane == e, wrow_ref[...], 0.0), axis=1, keepdims=True)
            for s in range(SUBLANES):
                lo, hi = _unpack_tokens(stage_ref[pl.ds(s, BM_EXP, stride=SUBLANES), :])
                lhs_ref[:, s * LANES:(s + 1) * LANES] = lo
                lhs_ref[:, HALF + s * LANES:HALF + (s + 1) * LANES] = hi
            gu = _dot(lhs_ref[...], wcat_ref[...])
            anchor = jnp.max(jnp.max(next_wrow[...], axis=0, keepdims=True), axis=1, keepdims=True) * 0.0
            hbuf_ref[parity] = (jax.nn.silu(gu[:, :D_EXPERT]) * gu[:, D_EXPERT:] * (wcol + anchor)).astype(BF16)
            h_copy(g).start()

        def block(j, carry):
            g = first + j

            @pl.when(g >= 2)
            def _():
                h_copy(g - 2).wait()

            for parity in (0, 1):
                @pl.when((g & 1) == parity)
                def _(parity=parity):
                    _gather_rows(g + 1, off_ref, xpk_ref, wdense_ref, *stages[1 - parity])
                    compute(g, parity)
            return carry

        lax.fori_loop(0, nblk, block, 0)

    @pl.when(e == N_EXPERTS - 1)
    def _():
        total = first + nblk

        @pl.when(total >= 2)
        def _():
            h_copy(total - 2).wait()

        @pl.when(total >= 1)
        def _():
            h_copy(total - 1).wait()

        hbuf_ref[0] = jnp.zeros((BM_EXP, D_EXPERT), BF16)

        def zero_block(j, carry):
            copy = pltpu.make_async_copy(hbuf_ref.at[0], h_hbm.at[_block_rows(0, j)], sem.at[0])
            copy.start()
            copy.wait()
            return carry

        lax.fori_loop(first + nblk, h_hbm.shape[0] // BM_EXP, zero_block, 0)


def _expert_gate_up(bstart, nblk, slot_off, xpk, wdense, exp_w_gate, exp_w_up, l):
    assert xpk.shape[0] & (xpk.shape[0] - 1) == 0
    grid_spec = pltpu.PrefetchScalarGridSpec(
        num_scalar_prefetch=3,
        grid=(N_EXPERTS,),
        in_specs=[
            pl.BlockSpec(xpk.shape, lambda e, *_: (0, 0), pipeline_mode=pl.Buffered(1)),
            pl.BlockSpec(wdense.shape, lambda e, *_: (0, 0), pipeline_mode=pl.Buffered(1)),
            pl.BlockSpec((None, None, D_MODEL, D_EXPERT), lambda e, *_: (l, e, 0, 0)),
            pl.BlockSpec((None, None, D_MODEL, D_EXPERT), lambda e, *_: (l, e, 0, 0)),
        ],
        out_specs=pl.BlockSpec(memory_space=pl.ANY),
        scratch_shapes=[
            pltpu.VMEM((BM_EXP * SUBLANES, LANES), U32),
            pltpu.VMEM((BM_EXP * SUBLANES, LANES), U32),
            pltpu.VMEM((BM_EXP, LANES), F32),
            pltpu.VMEM((BM_EXP, LANES), F32),
            pltpu.VMEM((BM_EXP, D_MODEL), BF16),
            pltpu.VMEM((D_MODEL, 2 * D_EXPERT), BF16),
            pltpu.VMEM((2, BM_EXP, D_EXPERT), BF16),
            pltpu.SemaphoreType.DMA((2,)),
        ],
    )
    return pl.pallas_call(
        _e1_kernel,
        grid_spec=grid_spec,
        out_shape=jax.ShapeDtypeStruct((slot_off.shape[0], D_EXPERT), BF16),
        compiler_params=pltpu.CompilerParams(vmem_limit_bytes=VMEM_LIMIT),
        name="expert_gate_up",
    )(bstart, nblk, slot_off, xpk, wdense, exp_w_gate, exp_w_up)


def _e2_kernel(bstart_ref, nblk_ref, off_ref, h_hbm, wd_ref, out_ref, ybuf_a, ybuf_b, hbuf_ref, wdb_ref, sem):
    e = pl.program_id(1)
    nblk = nblk_ref[e]
    first = bstart_ref[e]
    total = bstart_ref[N_EXPERTS - 1] + nblk_ref[N_EXPERTS - 1]
    group = SUBLANES
    chunks = HALF // LANES
    ybufs = (ybuf_a, ybuf_b)

    def h_copy(g):
        slot = g % H_SLOTS
        return pltpu.make_async_copy(h_hbm.at[_block_rows(0, g)], hbuf_ref.at[slot], sem.at[slot])

    def project(g, ybuf_ref):
        y = _dot(hbuf_ref[g % H_SLOTS], wdb_ref[...])
        for t in range(BM_EXP // SUBLANES):
            for c in range(chunks):
                t0 = (t * chunks + c) * SUBLANES
                ybuf_ref[t0:t0 + SUBLANES, :] = y[t * SUBLANES:(t + 1) * SUBLANES, c * LANES:(c + 1) * LANES]

    def scatter(g, ybuf_ref):
        base = g * BM_EXP
        for g0 in range(0, BM_EXP, group):
            dsts, vals = [], []
            for r in range(g0, g0 + group):
                dst = pl.ds(pl.multiple_of(off_ref[base + r], SUBLANES), SUBLANES)
                row = ybuf_ref[pl.ds((r // SUBLANES) * chunks * SUBLANES + r % SUBLANES, chunks, stride=SUBLANES), :]
                dsts.append(dst)
                vals.append(out_ref[dst, :] + row)
            for dst, val in zip(dsts, vals):
                out_ref[dst, :] = val

    @pl.when(e == 0)
    def _():
        out_ref[...] = jnp.zeros_like(out_ref)
        for g in range(H_SLOTS):
            @pl.when(g <= total)
            def _(g=g):
                h_copy(g).start()

    @pl.when(nblk > 0)
    def _():
        wdb_ref[...] = wd_ref[...].astype(BF16)

        @pl.when(first == 0)
        def _():
            h_copy(0).wait()

        for parity in (0, 1):
            @pl.when((first & 1) == parity)
            def _(parity=parity):
                project(first, ybufs[parity])

        def block(j, carry):
            g = first + j
            h_copy(g + 1).wait()

            for parity in (0, 1):
                @pl.when((g & 1) == parity)
                def _(parity=parity):
                    project(g + 1, ybufs[1 - parity])
                    scatter(g, ybufs[parity])

            @pl.when(g + H_SLOTS <= total)
            def _():
                h_copy(g + H_SLOTS).start()
            return carry

        lax.fori_loop(0, nblk, block, 0)


def _expert_down_combine(bstart, nblk, slot_off, hmid, exp_w_down, l, n):
    rows = (n + 1) * SUBLANES
    grid_spec = pltpu.PrefetchScalarGridSpec(
        num_scalar_prefetch=3,
        grid=(D_MODEL // HALF, N_EXPERTS),
        in_specs=[
            pl.BlockSpec(memory_space=pl.ANY),
            pl.BlockSpec((None, None, D_EXPERT, HALF), lambda c, e, *_: (l, e, 0, c)),
        ],
        out_specs=pl.BlockSpec((None, rows, LANES), lambda c, e, *_: (c, 0, 0),
                               pipeline_mode=pl.Buffered(1)),
        scratch_shapes=[
            pltpu.VMEM((BM_EXP * HALF // LANES, LANES), F32),
            pltpu.VMEM((BM_EXP * HALF // LANES, LANES), F32),
            pltpu.VMEM((H_SLOTS, BM_EXP, D_EXPERT), BF16),
            pltpu.VMEM((D_EXPERT, HALF), BF16),
            pltpu.SemaphoreType.DMA((H_SLOTS,)),
        ],
    )
    return pl.pallas_call(
        _e2_kernel,
        grid_spec=grid_spec,
        out_shape=jax.ShapeDtypeStruct((D_MODEL // HALF, rows, LANES), F32),
        compiler_params=pltpu.CompilerParams(vmem_limit_bytes=VMEM_LIMIT),
        name="expert_down_combine",
    )(bstart, nblk, slot_off, hmid, exp_w_down)


def _final_kernel(x1_ref, routed_ref, wgu_ref, wd_ref, g_ref, b_ref, x2_ref, x2bf_ref):
    ts = x1_ref.shape[0]
    gu = _dot(x1_ref[...].astype(BF16), wgu_ref[...])
    hs = (jax.nn.silu(gu[:, :D_SHARED]) * gu[:, D_SHARED:]).astype(BF16)
    shared = _dot(hs, wd_ref[...])
    routed = jnp.concatenate(
        [routed_ref[c, pl.ds(s, ts, stride=SUBLANES), :]
         for c in range(D_MODEL // HALF) for s in range(SUBLANES)], axis=1)
    z = ALPHA * x1_ref[...] + (shared + routed)
    x2 = _ln(z, g_ref[...], b_ref[...])
    x2_ref[...] = x2
    x2bf_ref[...] = x2.astype(BF16)


def _shared_ln2(x1, routed, sh_gu_bf, sh_d_bf, ln2_g, ln2_b, l):
    n = x1.shape[0]
    row_spec = pl.BlockSpec((TS_FIN, D_MODEL), lambda i: (i, 0))
    return pl.pallas_call(
        _final_kernel,
        grid=(n // TS_FIN,),
        in_specs=[
            row_spec,
            pl.BlockSpec((D_MODEL // HALF, TS_FIN * SUBLANES, LANES), lambda i: (0, i, 0)),
            pl.BlockSpec((None, D_MODEL, 2 * D_SHARED), lambda i: (l, 0, 0)),
            pl.BlockSpec((None, D_SHARED, D_MODEL), lambda i: (l, 0, 0)),
            pl.BlockSpec((None, 1, D_MODEL), lambda i: (l, 0, 0)),
            pl.BlockSpec((None, 1, D_MODEL), lambda i: (l, 0, 0)),
        ],
        out_specs=[row_spec, row_spec],
        out_shape=[jax.ShapeDtypeStruct((n, D_MODEL), F32), jax.ShapeDtypeStruct((n, D_MODEL), BF16)],
        compiler_params=pltpu.CompilerParams(vmem_limit_bytes=VMEM_LIMIT),
        name="shared_ln2",
    )(x1, routed, sh_gu_bf, sh_d_bf, ln2_g, ln2_b)


def kernel(x, mem, mem_ln_g, mem_ln_b, w_in, sgu_ln_g, sgu_ln_b, sgu_w, sgu_b, pool_w, pool_scale, conv_w, conv_b, conv_ln_g, conv_ln_b, w_mem_kv, w_out, ln1_g, ln1_b, w_router, router_bias, exp_w_gate, exp_w_up, exp_w_down, sh_w_gate, sh_w_up, sh_w_down, ln2_g, ln2_b):
    batch, seq, d = x.shape
    n = batch * seq
    depth = w_in.shape[0]
    assert (d, depth) == (D_MODEL, DEPTH) and seq % TS_MIX == 0 and n % BM_IN == 0 and n % T_ROUTE == 0

    w_in_p = jnp.concatenate([w_in[..., :2 * GW], w_in[..., 5 * GW:], w_in[..., 2 * GW:5 * GW]], axis=-1).astype(BF16)
    wout_bf = w_out.astype(BF16)
    sh_gu_bf = jnp.concatenate([sh_w_gate, sh_w_up], axis=-1).astype(BF16)
    sh_d_bf = sh_w_down.astype(BF16)
    w_router_t = jnp.swapaxes(w_router, 1, 2)
    bias3 = router_bias[..., None]

    def row(a):
        return a[:, None, :]

    p = dict(sgu_w=sgu_w, sgu_b=sgu_b, sgu_ln_g=row(sgu_ln_g), sgu_ln_b=row(sgu_ln_b), pool_w=pool_w,
             pool_scale=row(pool_scale), conv_w=conv_w, conv_b=row(conv_b), conv_ln_g=row(conv_ln_g),
             conv_ln_b=row(conv_ln_b), ln1_g=row(ln1_g), ln1_b=row(ln1_b))
    ln2_g3, ln2_b3 = row(ln2_g), row(ln2_b)

    kv = _memory_kv(mem.reshape(batch * N_MEM, d), mem_ln_g[None, :], mem_ln_b[None, :], w_mem_kv)

    nb = n * TOP_K // BM_EXP + N_EXPERTS
    assert (1 << BM_SHIFT) == BM_EXP

    xf = x.reshape(n, d)
    xbf = xf.astype(BF16)
    for l in range(depth):
        h6 = _in_proj(xbf, w_in_p, l)
        x1, xpk = _mixers(h6, kv, xf, wout_bf, p, l, batch, seq)
        dest, wdense, emeta = _router(x1, w_router_t, bias3, l)
        slot_off = _slot_table(dest.reshape(-1), emeta, n, nb)
        bstart, nblk = emeta[2, :N_EXPERTS], emeta[3, :N_EXPERTS]
        hmid = _expert_gate_up(bstart, nblk, slot_off, xpk, wdense, exp_w_gate, exp_w_up, l)
        routed = _expert_down_combine(bstart, nblk, slot_off, hmid, exp_w_down, l, n)
        xf, xbf = _shared_ln2(x1, routed, sh_gu_bf, sh_d_bf, ln2_g3, ln2_b3, l)
    return xf.reshape(batch, seq, d)
```

```python
import functools

import jax
import jax.numpy as jnp
from jax import lax
from jax.experimental import pallas as pl
from jax.experimental.pallas import tpu as pltpu

D_MODEL = 2048
DEPTH = 4
GW = 512
HEAD_DIM = 128
HEADS = 4
CHUNK = 128
POOL_WINDOWS = (2, 4, 8, 16)
CONV_WIDTH = 31
N_MEM = 256
N_EXPERTS = 64
N_GROUPS = 8
GROUP_SIZE = N_EXPERTS // N_GROUPS
TOPK_GROUPS = 4
TOP_K = 8
D_EXPERT = 256
D_SHARED = 256
ROUTED_SCALE = 2.5
LN_EPS = 1e-5
ALPHA = (2.0 * DEPTH) ** 0.25

LANES = 128
SUBLANES = 8
HALF = D_MODEL // 2
HALO = 32
VMEM_LIMIT = 56 * 1024 * 1024

TS_MIX = 256
BM_IN = 1024
BN_IN = 1024
T_ROUTE = 512
BM_EXP = 256
BM_SHIFT = 8
TS_FIN = 512
SLOT_UNROLL = 16
H_SLOTS = 4

F32 = jnp.float32
BF16 = jnp.bfloat16
U32 = jnp.uint32
I32 = jnp.int32


def _ln(x, g, b):
    mu = jnp.mean(x, axis=-1, keepdims=True)
    xc = x - mu
    var = jnp.mean(xc * xc, axis=-1, keepdims=True)
    return xc * lax.rsqrt(var + LN_EPS) * g + b


def _dot(a, b):
    return jnp.dot(a, b, preferred_element_type=F32)


def _dot_nt(a, b):
    return lax.dot_general(a, b, (((1,), (1,)), ((), ())), preferred_element_type=F32)


def _kv_kernel(mem_ref, g_ref, b_ref, w_ref, o_ref):
    memn = _ln(mem_ref[...], g_ref[...], b_ref[...])
    o_ref[...] = _dot(memn.astype(BF16), w_ref[...].astype(BF16)).astype(BF16)


def _memory_kv(mem2, g, b, w_mem_kv):
    nb = mem2.shape[0] // N_MEM
    return pl.pallas_call(
        _kv_kernel,
        grid=(DEPTH, nb),
        in_specs=[
            pl.BlockSpec((N_MEM, D_MODEL), lambda l, i: (i, 0)),
            pl.BlockSpec((1, D_MODEL), lambda l, i: (0, 0)),
            pl.BlockSpec((1, D_MODEL), lambda l, i: (0, 0)),
            pl.BlockSpec((None, D_MODEL, 2 * GW), lambda l, i: (l, 0, 0)),
        ],
        out_specs=pl.BlockSpec((None, N_MEM, 2 * GW), lambda l, i: (l, i, 0)),
        out_shape=jax.ShapeDtypeStruct((DEPTH, mem2.shape[0], 2 * GW), BF16),
        compiler_params=pltpu.CompilerParams(vmem_limit_bytes=VMEM_LIMIT),
        name="memory_kv",
    )(mem2, g, b, w_mem_kv)


def _inproj_kernel(x_ref, w_ref, o_ref):
    acc = _dot(x_ref[...], w_ref[...])
    o_ref[0] = acc[:, :GW]
    o_ref[1] = acc[:, GW:]


def _in_proj(xbf, w_in_p, l):
    n = xbf.shape[0]
    return pl.pallas_call(
        _inproj_kernel,
        grid=(6 * GW // BN_IN, n // BM_IN),
        in_specs=[
            pl.BlockSpec((BM_IN, D_MODEL), lambda j, i: (i, 0)),
            pl.BlockSpec((None, D_MODEL, BN_IN), lambda j, i: (l, 0, j)),
        ],
        out_specs=pl.BlockSpec((BN_IN // GW, BM_IN, GW), lambda j, i: (j, i, 0)),
        out_shape=jax.ShapeDtypeStruct((6, n, GW), F32),
        compiler_params=pltpu.CompilerParams(vmem_limit_bytes=VMEM_LIMIT),
        name="in_proj",
    )(xbf, w_in_p)


def _pack_tokens(x, xpk_ref):
    rows = x.shape[0]
    for s in range(SUBLANES):
        lo = x[:, s * LANES:(s + 1) * LANES]
        hi = x[:, HALF + s * LANES:HALF + (s + 1) * LANES]
        xpk_ref[pl.ds(s, rows, stride=SUBLANES), :] = pltpu.pack_elementwise([lo, hi], packed_dtype=BF16)


def _unpack_tokens(word):
    lo = pltpu.unpack_elementwise(word, index=0, packed_dtype=BF16, unpacked_dtype=F32)
    hi = pltpu.unpack_elementwise(word, index=1, packed_dtype=BF16, unpacked_dtype=F32)
    return lo.astype(BF16), hi.astype(BF16)


def _mix_kernel(h_ref, halo_ref, kv_ref, x_ref, wout_ref, sguw_ref, sgub_ref, sgug_ref, sgubeta_ref,
                poolw_ref, pools_ref, convw_ref, convb_ref, cvg_ref, cvb_ref, ln1g_ref, ln1b_ref,
                x1_ref, xpk_ref, mix_ref, cbuf_ref):
    i = pl.program_id(1)
    ts = h_ref.shape[1]
    first = i == 0

    u = jax.nn.gelu(h_ref[0])
    v = jax.nn.gelu(h_ref[1])
    tri = (lax.broadcasted_iota(I32, (CHUNK, CHUNK), 0) >= lax.broadcasted_iota(I32, (CHUNK, CHUNK), 1))
    for hd in range(HEADS):
        cs = slice(hd * HEAD_DIM, (hd + 1) * HEAD_DIM)
        vn = _ln(v[:, cs], sgug_ref[:, cs], sgubeta_ref[:, cs]).astype(BF16)
        wm = jnp.where(tri, sguw_ref[hd], 0.0).astype(BF16)
        bcol = sgub_ref[:, hd:hd + 1]
        for c in range(ts // CHUNK):
            rs = slice(c * CHUNK, (c + 1) * CHUNK)
            mixed = _dot(wm, vn[rs]) + bcol
            mix_ref[rs, cs] = (u[rs, cs] * mixed).astype(BF16)

    hp = h_ref[3]
    halo_p = jnp.where(first, 0.0, halo_ref[0])
    ext = jnp.concatenate([halo_p, hp], axis=0)
    pos1 = (i * ts + lax.broadcasted_iota(I32, (ts, 1), 0) + 1).astype(F32)
    for g, win in enumerate(POOL_WINDOWS):
        cs = slice(g * LANES, (g + 1) * LANES)
        s = ext[:, cs]
        sh = 1
        while sh < win:
            s = s + pltpu.roll(s, sh, 0)
            sh *= 2
        cnt = jnp.minimum(pos1, float(win))
        pooled = s[HALO:] / cnt - hp[:, cs]
        y = _dot(pooled.astype(BF16), poolw_ref[g].astype(BF16)) * pools_ref[:, cs]
        mix_ref[:, GW + g * LANES:GW + (g + 1) * LANES] = y.astype(BF16)

    glu_ext = jnp.concatenate([jnp.where(first, 0.0, halo_ref[1] * jax.nn.sigmoid(halo_ref[2])),
                               h_ref[4] * jax.nn.sigmoid(h_ref[5])], axis=0)
    cbuf_ref[0] = glu_ext
    for r in range(1, SUBLANES):
        cbuf_ref[r] = pltpu.roll(glu_ext, ts + HALO - r, 0)
    rchunk = 64
    for g in range(GW // LANES):
        cs = slice(g * LANES, (g + 1) * LANES)
        parts = []
        for r0 in range(0, ts, rchunk):
            acc = jnp.zeros((rchunk, LANES), F32)
            for j in range(CONV_WIDTH):
                lead = HALO - (CONV_WIDTH - 1) + j
                start = r0 + lead - lead % SUBLANES
                acc = acc + convw_ref[j:j + 1, cs] * cbuf_ref[lead % SUBLANES, start:start + rchunk, cs]
            parts.append(acc)
        y = jnp.concatenate(parts, axis=0) + convb_ref[:, cs]
        y = _ln(y, cvg_ref[:, cs], cvb_ref[:, cs])
        mix_ref[:, 2 * GW + g * LANES:2 * GW + (g + 1) * LANES] = jax.nn.silu(y).astype(BF16)

    q = h_ref[2]
    for hd in range(HEADS):
        cs = slice(hd * HEAD_DIM, (hd + 1) * HEAD_DIM)
        kh = kv_ref[:, hd * HEAD_DIM:(hd + 1) * HEAD_DIM]
        vh = kv_ref[:, GW + hd * HEAD_DIM:GW + (hd + 1) * HEAD_DIM]
        sc = _dot_nt(q[:, cs].astype(BF16), kh) * (HEAD_DIM ** -0.5)
        e = jnp.exp(sc - jnp.max(sc, axis=-1, keepdims=True))
        pr = e * (1.0 / jnp.sum(e, axis=-1, keepdims=True))
        mix_ref[:, 3 * GW + hd * HEAD_DIM:3 * GW + (hd + 1) * HEAD_DIM] = _dot(pr.astype(BF16), vh).astype(BF16)

    z = ALPHA * x_ref[...] + _dot(mix_ref[...], wout_ref[...])
    x1 = _ln(z, ln1g_ref[...], ln1b_ref[...])
    x1_ref[...] = x1
    _pack_tokens(x1, xpk_ref)


def _mixers(h6, kv, x, wout_bf, p, l, batch, seq):
    n = x.shape[0]
    ns = seq // TS_MIX
    hb = TS_MIX // HALO

    def vec(width):
        return pl.BlockSpec((None, 1, width), lambda b, i: (l, 0, 0))

    return pl.pallas_call(
        _mix_kernel,
        grid=(batch, ns),
        in_specs=[
            pl.BlockSpec((6, TS_MIX, GW), lambda b, i: (0, b * ns + i, 0)),
            pl.BlockSpec((3, HALO, GW), lambda b, i: (1, jnp.maximum((b * ns + i) * hb - 1, 0), 0)),
            pl.BlockSpec((None, N_MEM, 2 * GW), lambda b, i: (l, b, 0)),
            pl.BlockSpec((TS_MIX, D_MODEL), lambda b, i: (b * ns + i, 0)),
            pl.BlockSpec((None, D_MODEL, D_MODEL), lambda b, i: (l, 0, 0)),
            pl.BlockSpec((None, HEADS, CHUNK, CHUNK), lambda b, i: (l, 0, 0, 0)),
            pl.BlockSpec((None, CHUNK, HEADS), lambda b, i: (l, 0, 0)),
            vec(GW), vec(GW),
            pl.BlockSpec((None, len(POOL_WINDOWS), LANES, LANES), lambda b, i: (l, 0, 0, 0)),
            vec(GW),
            pl.BlockSpec((None, CONV_WIDTH, GW), lambda b, i: (l, 0, 0)),
            vec(GW), vec(GW), vec(GW), vec(D_MODEL), vec(D_MODEL),
        ],
        out_specs=[
            pl.BlockSpec((TS_MIX, D_MODEL), lambda b, i: (b * ns + i, 0)),
            pl.BlockSpec((TS_MIX * SUBLANES, LANES), lambda b, i: (b * ns + i, 0)),
        ],
        out_shape=[
            jax.ShapeDtypeStruct((n, D_MODEL), F32),
            jax.ShapeDtypeStruct((n * SUBLANES, LANES), U32),
        ],
        scratch_shapes=[
            pltpu.VMEM((TS_MIX, D_MODEL), BF16),
            pltpu.VMEM((SUBLANES, TS_MIX + HALO, GW), F32),
        ],
        compiler_params=pltpu.CompilerParams(vmem_limit_bytes=VMEM_LIMIT),
        name="mixers_outproj_ln1",
    )(h6, h6, kv, x, wout_bf, p['sgu_w'], p['sgu_b'], p['sgu_ln_g'], p['sgu_ln_b'],
      p['pool_w'], p['pool_scale'], p['conv_w'], p['conv_b'], p['conv_ln_g'], p['conv_ln_b'],
      p['ln1_g'], p['ln1_b'])


def _split_bf16(a):
    hi = a.astype(BF16)
    lo = (a - hi.astype(F32)).astype(BF16)
    return hi, lo


def _router_kernel(x_ref, wrt_ref, bias_ref, dest_ref, wdense_ref, emeta_ref, carry_ref, idx_ref):
    t = x_ref.shape[0]
    neg = -jnp.inf
    step = pl.program_id(0)

    @pl.when(step == 0)
    def _():
        carry_ref[...] = jnp.zeros_like(carry_ref)

    xh, xl = _split_bf16(x_ref[...])
    wh, wl = _split_bf16(wrt_ref[...])
    logits = _dot_nt(wh, xh) + (_dot_nt(wh, xl) + _dot_nt(wl, xh))
    scores = jax.nn.sigmoid(logits)
    choice = scores + bias_ref[...]

    iota8 = lax.broadcasted_iota(I32, (GROUP_SIZE, t), 0)
    rows = []
    for g in range(N_GROUPS):
        cg = choice[g * GROUP_SIZE:(g + 1) * GROUP_SIZE]
        m1 = jnp.max(cg, axis=0, keepdims=True)
        i1 = jnp.min(jnp.where(cg == m1, iota8, GROUP_SIZE), axis=0, keepdims=True)
        m2 = jnp.max(jnp.where(iota8 == i1, neg, cg), axis=0, keepdims=True)
        rows.append(m1 + m2)
    gwork = jnp.concatenate(rows, axis=0)

    gsel = jnp.zeros((N_GROUPS, t), jnp.bool_)
    for _ in range(TOPK_GROUPS):
        m = jnp.max(gwork, axis=0, keepdims=True)
        ii = jnp.min(jnp.where(gwork == m, iota8, N_GROUPS), axis=0, keepdims=True)
        hit = iota8 == ii
        gsel = gsel | hit
        gwork = jnp.where(hit, neg, gwork)
    emask = jnp.concatenate(
        [jnp.broadcast_to(gsel[g:g + 1], (GROUP_SIZE, t)) for g in range(N_GROUPS)], axis=0)

    iota_e = lax.broadcasted_iota(I32, (N_EXPERTS, t), 0)
    work = jnp.where(emask, choice, neg)
    sel = jnp.zeros((N_EXPERTS, t), jnp.bool_)
    idx_rows, raw_rows = [], []
    for _ in range(TOP_K):
        m = jnp.max(work, axis=0, keepdims=True)
        ii = jnp.min(jnp.where(work == m, iota_e, N_EXPERTS), axis=0, keepdims=True)
        hit = iota_e == ii
        idx_rows.append(ii)
        raw_rows.append(jnp.sum(jnp.where(hit, scores, 0.0), axis=0, keepdims=True))
        sel = sel | hit
        work = jnp.where(hit, neg, work)
    wsum = raw_rows[0]
    for r in raw_rows[1:]:
        wsum = wsum + r
    scale = ROUTED_SCALE / (wsum + 1e-20)

    dense = jnp.zeros((N_EXPERTS, t), F32)
    for k in range(TOP_K):
        dense = jnp.where(iota_e == idx_rows[k], raw_rows[k] * scale, dense)
    wdense_ref[...] = jnp.concatenate([dense, jnp.zeros((LANES - N_EXPERTS, t), F32)], axis=0).T

    upper = (lax.broadcasted_iota(I32, (t, t), 0) < lax.broadcasted_iota(I32, (t, t), 1))
    sel_f = sel.astype(F32)
    prefix = _dot(sel_f.astype(BF16), upper.astype(F32).astype(BF16)) + carry_ref[...]
    pos_rows = [jnp.sum(jnp.where(iota_e == idx_rows[k], prefix, 0.0), axis=0, keepdims=True)
                for k in range(TOP_K)]
    dest_ref[step] = jnp.concatenate(pos_rows, axis=0).astype(I32)
    idx_ref[step] = jnp.concatenate(idx_rows, axis=0)
    carry = carry_ref[...] + jnp.sum(sel_f, axis=1, keepdims=True)
    carry_ref[...] = carry

    @pl.when(step == pl.num_programs(0) - 1)
    def _():
        cnt = carry.astype(I32)
        nblk = (cnt + (BM_EXP - 1)) >> BM_SHIFT
        lower = (lax.broadcasted_iota(I32, (N_EXPERTS, N_EXPERTS), 0)
                 >= lax.broadcasted_iota(I32, (N_EXPERTS, N_EXPERTS), 1))
        nblk_wide = jnp.broadcast_to(nblk.astype(F32), (N_EXPERTS, LANES)).astype(BF16)
        bend = _dot(lower.astype(F32).astype(BF16), nblk_wide)[:, 0:1].astype(I32)
        bstart = bend - nblk
        pad_start = bstart << BM_SHIFT
        pad_rows = jnp.broadcast_to(pad_start, (N_EXPERTS, t))

        def add_start(tile, c):
            idx_t = idx_ref[tile]
            add = jnp.zeros((TOP_K, t), I32)
            for ex in range(N_EXPERTS):
                add = jnp.where(idx_t == ex, pad_rows[ex:ex + 1], add)
            dest_ref[tile] = dest_ref[tile] + add
            return c

        lax.fori_loop(0, dest_ref.shape[0], add_start, 0)

        eye = (lax.broadcasted_iota(I32, (N_EXPERTS, LANES), 0) == lax.broadcasted_iota(I32, (N_EXPERTS, LANES), 1))

        def as_row(col):
            return jnp.sum(jnp.where(eye, col, 0), axis=0, keepdims=True)

        emeta_ref[...] = jnp.concatenate(
            [as_row(pad_start + cnt), as_row(bend << BM_SHIFT), as_row(bstart), as_row(nblk),
             jnp.zeros((SUBLANES - 4, LANES), I32)], axis=0)


def _router(x1, w_router_t, bias, l):
    n = x1.shape[0]
    nt = n // T_ROUTE
    return pl.pallas_call(
        _router_kernel,
        grid=(nt,),
        in_specs=[
            pl.BlockSpec((T_ROUTE, D_MODEL), lambda t: (t, 0)),
            pl.BlockSpec((None, N_EXPERTS, D_MODEL), lambda t: (l, 0, 0)),
            pl.BlockSpec((None, N_EXPERTS, 1), lambda t: (l, 0, 0)),
        ],
        out_specs=[
            pl.BlockSpec((nt, TOP_K, T_ROUTE), lambda t: (0, 0, 0)),
            pl.BlockSpec((T_ROUTE, LANES), lambda t: (t, 0)),
            pl.BlockSpec((SUBLANES, LANES), lambda t: (0, 0)),
        ],
        out_shape=[
            jax.ShapeDtypeStruct((nt, TOP_K, T_ROUTE), I32),
            jax.ShapeDtypeStruct((n, LANES), F32),
            jax.ShapeDtypeStruct((SUBLANES, LANES), I32),
        ],
        scratch_shapes=[pltpu.VMEM((N_EXPERTS, 1), F32), pltpu.VMEM((nt, TOP_K, T_ROUTE), I32)],
        compiler_params=pltpu.CompilerParams(vmem_limit_bytes=VMEM_LIMIT),
        name="router",
    )(x1, w_router_t, bias)


def _slot_kernel(dest_ref, emeta_ref, off_ref, *, n_tok):
    spare = n_tok * SUBLANES
    group = SUBLANES

    def fill_range(start, end):
        def fill(j, c):
            for u in range(group):
                off_ref[start + j * group + u] = spare
            return c
        lax.fori_loop(0, (end - start + group - 1) // group, fill, 0)

    def fill_expert(ex, c):
        fill_range(emeta_ref[0, ex], emeta_ref[1, ex])
        return c

    lax.fori_loop(0, N_EXPERTS, fill_expert, 0)
    fill_range(emeta_ref[1, N_EXPERTS - 1], off_ref.shape[0])

    def tile_body(tile, c):
        for k in range(TOP_K):
            base = (tile * TOP_K + k) * T_ROUTE

            def inner(j, c2, base=base):
                src = base + j * SLOT_UNROLL
                val = (tile * T_ROUTE + j * SLOT_UNROLL) * SUBLANES
                for u in range(SLOT_UNROLL):
                    off_ref[dest_ref[src + u]] = val + u * SUBLANES
                return c2

            lax.fori_loop(0, T_ROUTE // SLOT_UNROLL, inner, 0)
        return c

    lax.fori_loop(0, n_tok // T_ROUTE, tile_body, 0)


def _slot_table(dest_flat, emeta, n, nb):
    smem = pl.BlockSpec(memory_space=pltpu.SMEM)
    return pl.pallas_call(
        functools.partial(_slot_kernel, n_tok=n),
        in_specs=[smem, smem],
        out_specs=smem,
        out_shape=jax.ShapeDtypeStruct(((nb + 1) * BM_EXP,), I32),
        name="slot_table",
    )(dest_flat, emeta)


def _block_rows(first_block, j):
    return pl.ds(pl.multiple_of((first_block + j) * BM_EXP, BM_EXP), BM_EXP)


def _gather_rows(g, off_ref, xpk_ref, wdense_ref, stage_ref, wrow_ref):
    row_mask = xpk_ref.shape[0] - 1
    base = g * BM_EXP
    for r in range(BM_EXP):
        off = off_ref[base + r] & row_mask
        stage_ref[r * SUBLANES:(r + 1) * SUBLANES, :] = xpk_ref[pl.ds(pl.multiple_of(off, SUBLANES), SUBLANES), :]
        wrow_ref[r:r + 1, :] = wdense_ref[pl.ds(off >> 3, 1), :]


def _e1_kernel(bstart_ref, nblk_ref, off_ref, xpk_ref, wdense_ref, wg_ref, wu_ref, h_hbm,
               stage_a, stage_b, wrow_a, wrow_b, lhs_ref, wcat_ref, hbuf_ref, sem):
    e = pl.program_id(0)
    nblk = nblk_ref[e]
    first = bstart_ref[e]
    stages = ((stage_a, wrow_a), (stage_b, wrow_b))

    def h_copy(g):
        return pltpu.make_async_copy(hbuf_ref.at[g & 1], h_hbm.at[_block_rows(0, g)], sem.at[g & 1])

    @pl.when(e == 0)
    def _():
        _gather_rows(0, off_ref, xpk_ref, wdense_ref, stage_a, wrow_a)

    @pl.when(nblk > 0)
    def _():
        wcat_ref[:, :D_EXPERT] = wg_ref[...].astype(BF16)
        wcat_ref[:, D_EXPERT:] = wu_ref[...].astype(BF16)
        lane = lax.broadcasted_iota(I32, (BM_EXP, LANES), 1)

        def compute(g, parity):
            stage_ref, wrow_ref = stages[parity]
            next_stage, next_wrow = stages[1 - parity]
            wcol = jnp.sum(jnp.where(lane == e, wrow_ref[...], 0.0), axis=1, keepdims=True)
            for s in range(SUBLANES):
                lo, hi = _unpack_tokens(stage_ref[pl.ds(s, BM_EXP, stride=SUBLANES), :])
                lhs_ref[:, s * LANES:(s + 1) * LANES] = lo
                lhs_ref[:, HALF + s * LANES:HALF + (s + 1) * LANES] = hi
            gu = _dot(lhs_ref[...], wcat_ref[...])
            anchor = jnp.max(jnp.max(next_wrow[...], axis=0, keepdims=True), axis=1, keepdims=True) * 0.0
            hbuf_ref[parity] = (jax.nn.silu(gu[:, :D_EXPERT]) * gu[:, D_EXPERT:] * (wcol + anchor)).astype(BF16)
            h_copy(g).start()

        def block(j, carry):
            g = first + j

            @pl.when(g >= 2)
            def _():
                h_copy(g - 2).wait()

            for parity in (0, 1):
                @pl.when((g & 1) == parity)
                def _(parity=parity):
                    _gather_rows(g + 1, off_ref, xpk_ref, wdense_ref, *stages[1 - parity])
                    compute(g, parity)
            return carry

        lax.fori_loop(0, nblk, block, 0)

    @pl.when(e == N_EXPERTS - 1)
    def _():
        total = first + nblk

        @pl.when(total >= 2)
        def _():
            h_copy(total - 2).wait()

        @pl.when(total >= 1)
        def _():
            h_copy(total - 1).wait()

        hbuf_ref[0] = jnp.zeros((BM_EXP, D_EXPERT), BF16)

        def zero_block(j, carry):
            copy = pltpu.make_async_copy(hbuf_ref.at[0], h_hbm.at[_block_rows(0, j)], sem.at[0])
            copy.start()
            copy.wait()
            return carry

        lax.fori_loop(first + nblk, h_hbm.shape[0] // BM_EXP, zero_block, 0)


def _expert_gate_up(bstart, nblk, slot_off, xpk, wdense, exp_w_gate, exp_w_up, l):
    assert xpk.shape[0] & (xpk.shape[0] - 1) == 0
    grid_spec = pltpu.PrefetchScalarGridSpec(
        num_scalar_prefetch=3,
        grid=(N_EXPERTS,),
        in_specs=[
            pl.BlockSpec(xpk.shape, lambda e, *_: (0, 0), pipeline_mode=pl.Buffered(1)),
            pl.BlockSpec(wdense.shape, lambda e, *_: (0, 0), pipeline_mode=pl.Buffered(1)),
            pl.BlockSpec((None, None, D_MODEL, D_EXPERT), lambda e, *_: (l, e, 0, 0)),
            pl.BlockSpec((None, None, D_MODEL, D_EXPERT), lambda e, *_: (l, e, 0, 0)),
        ],
        out_specs=pl.BlockSpec(memory_space=pl.ANY),
        scratch_shapes=[
            pltpu.VMEM((BM_EXP * SUBLANES, LANES), U32),
            pltpu.VMEM((BM_EXP * SUBLANES, LANES), U32),
            pltpu.VMEM((BM_EXP, LANES), F32),
            pltpu.VMEM((BM_EXP, LANES), F32),
            pltpu.VMEM((BM_EXP, D_MODEL), BF16),
            pltpu.VMEM((D_MODEL, 2 * D_EXPERT), BF16),
            pltpu.VMEM((2, BM_EXP, D_EXPERT), BF16),
            pltpu.SemaphoreType.DMA((2,)),
        ],
    )
    return pl.pallas_call(
        _e1_kernel,
        grid_spec=grid_spec,
        out_shape=jax.ShapeDtypeStruct((slot_off.shape[0], D_EXPERT), BF16),
        compiler_params=pltpu.CompilerParams(vmem_limit_bytes=VMEM_LIMIT),
        name="expert_gate_up",
    )(bstart, nblk, slot_off, xpk, wdense, exp_w_gate, exp_w_up)


def _e2_kernel(bstart_ref, nblk_ref, off_ref, h_hbm, wd_ref, out_ref, ybuf_a, ybuf_b, hbuf_ref, wdb_ref, sem):
    e = pl.program_id(1)
    nblk = nblk_ref[e]
    first = bstart_ref[e]
    total = bstart_ref[N_EXPERTS - 1] + nblk_ref[N_EXPERTS - 1]
    group = SUBLANES
    chunks = HALF // LANES
    ybufs = (ybuf_a, ybuf_b)

    def h_copy(g):
        slot = g % H_SLOTS
        return pltpu.make_async_copy(h_hbm.at[_block_rows(0, g)], hbuf_ref.at[slot], sem.at[slot])

    def project(g, ybuf_ref):
        y = _dot(hbuf_ref[g % H_SLOTS], wdb_ref[...])
        for t in range(BM_EXP // SUBLANES):
            for c in range(chunks):
                t0 = (t * chunks + c) * SUBLANES
                ybuf_ref[t0:t0 + SUBLANES, :] = y[t * SUBLANES:(t + 1) * SUBLANES, c * LANES:(c + 1) * LANES]

    def scatter(g, ybuf_ref):
        base = g * BM_EXP
        for g0 in range(0, BM_EXP, group):
            dsts, vals = [], []
            for r in range(g0, g0 + group):
                dst = pl.ds(pl.multiple_of(off_ref[base + r], SUBLANES), SUBLANES)
                row = ybuf_ref[pl.ds((r // SUBLANES) * chunks * SUBLANES + r % SUBLANES, chunks, stride=SUBLANES), :]
                dsts.append(dst)
                vals.append(out_ref[dst, :] + row)
            for dst, val in zip(dsts, vals):
                out_ref[dst, :] = val

    @pl.when(e == 0)
    def _():
        out_ref[...] = jnp.zeros_like(out_ref)
        for g in range(H_SLOTS):
            @pl.when(g <= total)
            def _(g=g):
                h_copy(g).start()

    @pl.when(nblk > 0)
    def _():
        wdb_ref[...] = wd_ref[...].astype(BF16)

        @pl.when(first == 0)
        def _():
            h_copy(0).wait()

        for parity in (0, 1):
            @pl.when((first & 1) == parity)
            def _(parity=parity):
                project(first, ybufs[parity])

        def block(j, carry):
            g = first + j
            h_copy(g + 1).wait()

            for parity in (0, 1):
                @pl.when((g & 1) == parity)
                def _(parity=parity):
                    project(g + 1, ybufs[1 - parity])
                    scatter(g, ybufs[parity])

            @pl.when(g + H_SLOTS <= total)
            def _():
                h_copy(g + H_SLOTS).start()
            return carry

        lax.fori_loop(0, nblk, block, 0)


def _expert_down_combine(bstart, nblk, slot_off, hmid, exp_w_down, l, n):
    rows = (n + 1) * SUBLANES
    grid_spec = pltpu.PrefetchScalarGridSpec(
        num_scalar_prefetch=3,
        grid=(D_MODEL // HALF, N_EXPERTS),
        in_specs=[
            pl.BlockSpec(memory_space=pl.ANY),
            pl.BlockSpec((None, None, D_EXPERT, HALF), lambda c, e, *_: (l, e, 0, c)),
        ],
        out_specs=pl.BlockSpec((None, rows, LANES), lambda c, e, *_: (c, 0, 0),
                               pipeline_mode=pl.Buffered(1)),
        scratch_shapes=[
            pltpu.VMEM((BM_EXP * HALF // LANES, LANES), F32),
            pltpu.VMEM((BM_EXP * HALF // LANES, LANES), F32),
            pltpu.VMEM((H_SLOTS, BM_EXP, D_EXPERT), BF16),
            pltpu.VMEM((D_EXPERT, HALF), BF16),
            pltpu.SemaphoreType.DMA((H_SLOTS,)),
        ],
    )
    return pl.pallas_call(
        _e2_kernel,
        grid_spec=grid_spec,
        out_shape=jax.ShapeDtypeStruct((D_MODEL // HALF, rows, LANES), F32),
        compiler_params=pltpu.CompilerParams(vmem_limit_bytes=VMEM_LIMIT),
        name="expert_down_combine",
    )(bstart, nblk, slot_off, hmid, exp_w_down)


def _final_kernel(x1_ref, routed_ref, wgu_ref, wd_ref, g_ref, b_ref, x2_ref, x2bf_ref):
    ts = x1_ref.shape[0]
    gu = _dot(x1_ref[...].astype(BF16), wgu_ref[...])
    hs = (jax.nn.silu(gu[:, :D_SHARED]) * gu[:, D_SHARED:]).astype(BF16)
    shared = _dot(hs, wd_ref[...])
    routed = jnp.concatenate(
        [routed_ref[c, pl.ds(s, ts, stride=SUBLANES), :]
         for c in range(D_MODEL // HALF) for s in range(SUBLANES)], axis=1)
    z = ALPHA * x1_ref[...] + (shared + routed)
    x2 = _ln(z, g_ref[...], b_ref[...])
    x2_ref[...] = x2
    x2bf_ref[...] = x2.astype(BF16)


def _shared_ln2(x1, routed, sh_gu_bf, sh_d_bf, ln2_g, ln2_b, l):
    n = x1.shape[0]
    row_spec = pl.BlockSpec((TS_FIN, D_MODEL), lambda i: (i, 0))
    return pl.pallas_call(
        _final_kernel,
        grid=(n // TS_FIN,),
        in_specs=[
            row_spec,
            pl.BlockSpec((D_MODEL // HALF, TS_FIN * SUBLANES, LANES), lambda i: (0, i, 0)),
            pl.BlockSpec((None, D_MODEL, 2 * D_SHARED), lambda i: (l, 0, 0)),
            pl.BlockSpec((None, D_SHARED, D_MODEL), lambda i: (l, 0, 0)),
            pl.BlockSpec((None, 1, D_MODEL), lambda i: (l, 0, 0)),
            pl.BlockSpec((None, 1, D_MODEL), lambda i: (l, 0, 0)),
        ],
        out_specs=[row_spec, row_spec],
        out_shape=[jax.ShapeDtypeStruct((n, D_MODEL), F32), jax.ShapeDtypeStruct((n, D_MODEL), BF16)],
        compiler_params=pltpu.CompilerParams(vmem_limit_bytes=VMEM_LIMIT),
        name="shared_ln2",
    )(x1, routed, sh_gu_bf, sh_d_bf, ln2_g, ln2_b)


def kernel(x, mem, mem_ln_g, mem_ln_b, w_in, sgu_ln_g, sgu_ln_b, sgu_w, sgu_b, pool_w, pool_scale, conv_w, conv_b, conv_ln_g, conv_ln_b, w_mem_kv, w_out, ln1_g, ln1_b, w_router, router_bias, exp_w_gate, exp_w_up, exp_w_down, sh_w_gate, sh_w_up, sh_w_down, ln2_g, ln2_b):
    batch, seq, d = x.shape
    n = batch * seq
    depth = w_in.shape[0]
    assert (d, depth) == (D_MODEL, DEPTH) and seq % TS_MIX == 0 and n % BM_IN == 0 and n % T_ROUTE == 0

    w_in_p = jnp.concatenate([w_in[..., :2 * GW], w_in[..., 5 * GW:], w_in[..., 2 * GW:5 * GW]], axis=-1).astype(BF16)
    wout_bf = w_out.astype(BF16)
    sh_gu_bf = jnp.concatenate([sh_w_gate, sh_w_up], axis=-1).astype(BF16)
    sh_d_bf = sh_w_down.astype(BF16)
    w_router_t = jnp.swapaxes(w_router, 1, 2)
    bias3 = router_bias[..., None]

    def row(a):
        return a[:, None, :]

    p = dict(sgu_w=sgu_w, sgu_b=sgu_b, sgu_ln_g=row(sgu_ln_g), sgu_ln_b=row(sgu_ln_b), pool_w=pool_w,
             pool_scale=row(pool_scale), conv_w=conv_w, conv_b=row(conv_b), conv_ln_g=row(conv_ln_g),
             conv_ln_b=row(conv_ln_b), ln1_g=row(ln1_g), ln1_b=row(ln1_b))
    ln2_g3, ln2_b3 = row(ln2_g), row(ln2_b)

    kv = _memory_kv(mem.reshape(batch * N_MEM, d), mem_ln_g[None, :], mem_ln_b[None, :], w_mem_kv)

    nb = n * TOP_K // BM_EXP + N_EXPERTS
    assert (1 << BM_SHIFT) == BM_EXP

    xf = x.reshape(n, d)
    xbf = xf.astype(BF16)
    for l in range(depth):
        h6 = _in_proj(xbf, w_in_p, l)
        x1, xpk = _mixers(h6, kv, xf, wout_bf, p, l, batch, seq)
        dest, wdense, emeta = _router(x1, w_router_t, bias3, l)
        slot_off = _slot_table(dest.reshape(-1), emeta, n, nb)
        bstart, nblk = emeta[2, :N_EXPERTS], emeta[3, :N_EXPERTS]
        hmid = _expert_gate_up(bstart, nblk, slot_off, xpk, wdense, exp_w_gate, exp_w_up, l)
        routed = _expert_down_combine(bstart, nblk, slot_off, hmid, exp_w_down, l, n)
        xf, xbf = _shared_ln2(x1, routed, sh_gu_bf, sh_d_bf, ln2_g3, ln2_b3, l)
    return xf.reshape(batch, seq, d)
```

```python
import functools

import jax
import jax.numpy as jnp
from jax import lax
from jax.experimental import pallas as pl
from jax.experimental.pallas import tpu as pltpu

D_MODEL = 2048
DEPTH = 4
GW = 512
HEAD_DIM = 128
HEADS = 4
CHUNK = 128
POOL_WINDOWS = (2, 4, 8, 16)
CONV_WIDTH = 31
N_MEM = 256
N_EXPERTS = 64
N_GROUPS = 8
GROUP_SIZE = N_EXPERTS // N_GROUPS
TOPK_GROUPS = 4
TOP_K = 8
D_EXPERT = 256
D_SHARED = 256
ROUTED_SCALE = 2.5
LN_EPS = 1e-5
ALPHA = (2.0 * DEPTH) ** 0.25

LANES = 128
SUBLANES = 8
HALF = D_MODEL // 2
HALO = 32
VMEM_LIMIT = 56 * 1024 * 1024

TS_MIX = 256
BM_IN = 1024
BN_IN = 1024
T_ROUTE = 512
BM_EXP = 256
BM_SHIFT = 8
TS_FIN = 512
SLOT_UNROLL = 16
H_SLOTS = 4
UNPACK_ROWS = 32

F32 = jnp.float32
BF16 = jnp.bfloat16
U32 = jnp.uint32
I32 = jnp.int32


def _ln(x, g, b):
    mu = jnp.mean(x, axis=-1, keepdims=True)
    xc = x - mu
    var = jnp.mean(xc * xc, axis=-1, keepdims=True)
    return xc * lax.rsqrt(var + LN_EPS) * g + b


def _dot(a, b):
    return jnp.dot(a, b, preferred_element_type=F32)


def _dot_nt(a, b):
    return lax.dot_general(a, b, (((1,), (1,)), ((), ())), preferred_element_type=F32)


def _kv_kernel(mem_ref, g_ref, b_ref, w_ref, o_ref):
    memn = _ln(mem_ref[...], g_ref[...], b_ref[...])
    o_ref[...] = _dot(memn.astype(BF16), w_ref[...].astype(BF16)).astype(BF16)


def _memory_kv(mem2, g, b, w_mem_kv):
    nb = mem2.shape[0] // N_MEM
    return pl.pallas_call(
        _kv_kernel,
        grid=(DEPTH, nb),
        in_specs=[
            pl.BlockSpec((N_MEM, D_MODEL), lambda l, i: (i, 0)),
            pl.BlockSpec((1, D_MODEL), lambda l, i: (0, 0)),
            pl.BlockSpec((1, D_MODEL), lambda l, i: (0, 0)),
            pl.BlockSpec((None, D_MODEL, 2 * GW), lambda l, i: (l, 0, 0)),
        ],
        out_specs=pl.BlockSpec((None, N_MEM, 2 * GW), lambda l, i: (l, i, 0)),
        out_shape=jax.ShapeDtypeStruct((DEPTH, mem2.shape[0], 2 * GW), BF16),
        compiler_params=pltpu.CompilerParams(vmem_limit_bytes=VMEM_LIMIT),
        name="memory_kv",
    )(mem2, g, b, w_mem_kv)


def _inproj_kernel(x_ref, w_ref, o_ref):
    acc = _dot(x_ref[...], w_ref[...])
    o_ref[0] = acc[:, :GW]
    o_ref[1] = acc[:, GW:]


def _in_proj(xbf, w_in_p, l):
    n = xbf.shape[0]
    return pl.pallas_call(
        _inproj_kernel,
        grid=(6 * GW // BN_IN, n // BM_IN),
        in_specs=[
            pl.BlockSpec((BM_IN, D_MODEL), lambda j, i: (i, 0)),
            pl.BlockSpec((None, D_MODEL, BN_IN), lambda j, i: (l, 0, j)),
        ],
        out_specs=pl.BlockSpec((BN_IN // GW, BM_IN, GW), lambda j, i: (j, i, 0)),
        out_shape=jax.ShapeDtypeStruct((6, n, GW), F32),
        compiler_params=pltpu.CompilerParams(vmem_limit_bytes=VMEM_LIMIT),
        name="in_proj",
    )(xbf, w_in_p)


def _pack_tokens(x, xpk_ref):
    rows = x.shape[0]
    for s in range(SUBLANES):
        lo = x[:, s * LANES:(s + 1) * LANES]
        hi = x[:, HALF + s * LANES:HALF + (s + 1) * LANES]
        xpk_ref[pl.ds(s, rows, stride=SUBLANES), :] = pltpu.pack_elementwise([lo, hi], packed_dtype=BF16)


def _unpack_tokens(word):
    lo = pltpu.unpack_elementwise(word, index=0, packed_dtype=BF16, unpacked_dtype=F32)
    hi = pltpu.unpack_elementwise(word, index=1, packed_dtype=BF16, unpacked_dtype=F32)
    return lo.astype(BF16), hi.astype(BF16)


def _mix_kernel(h_ref, halo_ref, kv_ref, x_ref, wout_ref, sguw_ref, sgub_ref, sgug_ref, sgubeta_ref,
                poolw_ref, pools_ref, convw_ref, convb_ref, cvg_ref, cvb_ref, ln1g_ref, ln1b_ref,
                x1_ref, xpk_ref, mix_ref, cbuf_ref):
    i = pl.program_id(1)
    ts = h_ref.shape[1]
    first = i == 0

    u = jax.nn.gelu(h_ref[0])
    v = jax.nn.gelu(h_ref[1])
    tri = (lax.broadcasted_iota(I32, (CHUNK, CHUNK), 0) >= lax.broadcasted_iota(I32, (CHUNK, CHUNK), 1))
    for hd in range(HEADS):
        cs = slice(hd * HEAD_DIM, (hd + 1) * HEAD_DIM)
        vn = _ln(v[:, cs], sgug_ref[:, cs], sgubeta_ref[:, cs]).astype(BF16)
        wm = jnp.where(tri, sguw_ref[hd], 0.0).astype(BF16)
        bcol = sgub_ref[:, hd:hd + 1]
        for c in range(ts // CHUNK):
            rs = slice(c * CHUNK, (c + 1) * CHUNK)
            mixed = _dot(wm, vn[rs]) + bcol
            mix_ref[rs, cs] = (u[rs, cs] * mixed).astype(BF16)

    hp = h_ref[3]
    halo_p = jnp.where(first, 0.0, halo_ref[0])
    ext = jnp.concatenate([halo_p, hp], axis=0)
    pos1 = (i * ts + lax.broadcasted_iota(I32, (ts, 1), 0) + 1).astype(F32)
    for g, win in enumerate(POOL_WINDOWS):
        cs = slice(g * LANES, (g + 1) * LANES)
        s = ext[:, cs]
        sh = 1
        while sh < win:
            s = s + pltpu.roll(s, sh, 0)
            sh *= 2
        cnt = jnp.minimum(pos1, float(win))
        pooled = s[HALO:] / cnt - hp[:, cs]
        y = _dot(pooled.astype(BF16), poolw_ref[g].astype(BF16)) * pools_ref[:, cs]
        mix_ref[:, GW + g * LANES:GW + (g + 1) * LANES] = y.astype(BF16)

    glu_ext = jnp.concatenate([jnp.where(first, 0.0, halo_ref[1] * jax.nn.sigmoid(halo_ref[2])),
                               h_ref[4] * jax.nn.sigmoid(h_ref[5])], axis=0)
    cbuf_ref[0] = glu_ext
    for r in range(1, SUBLANES):
        cbuf_ref[r] = pltpu.roll(glu_ext, ts + HALO - r, 0)
    rchunk = 64
    for g in range(GW // LANES):
        cs = slice(g * LANES, (g + 1) * LANES)
        parts = []
        for r0 in range(0, ts, rchunk):
            acc = jnp.zeros((rchunk, LANES), F32)
            for j in range(CONV_WIDTH):
                lead = HALO - (CONV_WIDTH - 1) + j
                start = r0 + lead - lead % SUBLANES
                acc = acc + convw_ref[j:j + 1, cs] * cbuf_ref[lead % SUBLANES, start:start + rchunk, cs]
            parts.append(acc)
        y = jnp.concatenate(parts, axis=0) + convb_ref[:, cs]
        y = _ln(y, cvg_ref[:, cs], cvb_ref[:, cs])
        mix_ref[:, 2 * GW + g * LANES:2 * GW + (g + 1) * LANES] = jax.nn.silu(y).astype(BF16)

    q = h_ref[2]
    for hd in range(HEADS):
        cs = slice(hd * HEAD_DIM, (hd + 1) * HEAD_DIM)
        kh = kv_ref[:, hd * HEAD_DIM:(hd + 1) * HEAD_DIM]
        vh = kv_ref[:, GW + hd * HEAD_DIM:GW + (hd + 1) * HEAD_DIM]
        sc = _dot_nt(q[:, cs].astype(BF16), kh) * (HEAD_DIM ** -0.5)
        e = jnp.exp(sc - jnp.max(sc, axis=-1, keepdims=True))
        pr = e * (1.0 / jnp.sum(e, axis=-1, keepdims=True))
        mix_ref[:, 3 * GW + hd * HEAD_DIM:3 * GW + (hd + 1) * HEAD_DIM] = _dot(pr.astype(BF16), vh).astype(BF16)

    z = ALPHA * x_ref[...] + _dot(mix_ref[...], wout_ref[...])
    x1 = _ln(z, ln1g_ref[...], ln1b_ref[...])
    x1_ref[...] = x1
    _pack_tokens(x1, xpk_ref)


def _mixers(h6, kv, x, wout_bf, p, l, batch, seq):
    n = x.shape[0]
    ns = seq // TS_MIX
    hb = TS_MIX // HALO

    def vec(width):
        return pl.BlockSpec((None, 1, width), lambda b, i: (l, 0, 0))

    return pl.pallas_call(
        _mix_kernel,
        grid=(batch, ns),
        in_specs=[
            pl.BlockSpec((6, TS_MIX, GW), lambda b, i: (0, b * ns + i, 0)),
            pl.BlockSpec((3, HALO, GW), lambda b, i: (1, jnp.maximum((b * ns + i) * hb - 1, 0), 0)),
            pl.BlockSpec((None, N_MEM, 2 * GW), lambda b, i: (l, b, 0)),
            pl.BlockSpec((TS_MIX, D_MODEL), lambda b, i: (b * ns + i, 0)),
            pl.BlockSpec((None, D_MODEL, D_MODEL), lambda b, i: (l, 0, 0)),
            pl.BlockSpec((None, HEADS, CHUNK, CHUNK), lambda b, i: (l, 0, 0, 0)),
            pl.BlockSpec((None, CHUNK, HEADS), lambda b, i: (l, 0, 0)),
            vec(GW), vec(GW),
            pl.BlockSpec((None, len(POOL_WINDOWS), LANES, LANES), lambda b, i: (l, 0, 0, 0)),
            vec(GW),
            pl.BlockSpec((None, CONV_WIDTH, GW), lambda b, i: (l, 0, 0)),
            vec(GW), vec(GW), vec(GW), vec(D_MODEL), vec(D_MODEL),
        ],
        out_specs=[
            pl.BlockSpec((TS_MIX, D_MODEL), lambda b, i: (b * ns + i, 0)),
            pl.BlockSpec((TS_MIX * SUBLANES, LANES), lambda b, i: (b * ns + i, 0)),
        ],
        out_shape=[
            jax.ShapeDtypeStruct((n, D_MODEL), F32),
            jax.ShapeDtypeStruct((n * SUBLANES, LANES), U32),
        ],
        scratch_shapes=[
            pltpu.VMEM((TS_MIX, D_MODEL), BF16),
            pltpu.VMEM((SUBLANES, TS_MIX + HALO, GW), F32),
        ],
        compiler_params=pltpu.CompilerParams(vmem_limit_bytes=VMEM_LIMIT),
        name="mixers_outproj_ln1",
    )(h6, h6, kv, x, wout_bf, p['sgu_w'], p['sgu_b'], p['sgu_ln_g'], p['sgu_ln_b'],
      p['pool_w'], p['pool_scale'], p['conv_w'], p['conv_b'], p['conv_ln_g'], p['conv_ln_b'],
      p['ln1_g'], p['ln1_b'])


def _split_bf16(a):
    hi = a.astype(BF16)
    lo = (a - hi.astype(F32)).astype(BF16)
    return hi, lo


def _router_kernel(x_ref, wrt_ref, bias_ref, dest_ref, wdense_ref, emeta_ref, carry_ref, idx_ref):
    t = x_ref.shape[0]
    neg = -jnp.inf
    step = pl.program_id(0)

    @pl.when(step == 0)
    def _():
        carry_ref[...] = jnp.zeros_like(carry_ref)

    xh, xl = _split_bf16(x_ref[...])
    wh, wl = _split_bf16(wrt_ref[...])
    logits = _dot_nt(wh, xh) + (_dot_nt(wh, xl) + _dot_nt(wl, xh))
    scores = jax.nn.sigmoid(logits)
    choice = scores + bias_ref[...]

    iota8 = lax.broadcasted_iota(I32, (GROUP_SIZE, t), 0)
    rows = []
    for g in range(N_GROUPS):
        cg = choice[g * GROUP_SIZE:(g + 1) * GROUP_SIZE]
        m1 = jnp.max(cg, axis=0, keepdims=True)
        i1 = jnp.min(jnp.where(cg == m1, iota8, GROUP_SIZE), axis=0, keepdims=True)
        m2 = jnp.max(jnp.where(iota8 == i1, neg, cg), axis=0, keepdims=True)
        rows.append(m1 + m2)
    gwork = jnp.concatenate(rows, axis=0)

    gsel = jnp.zeros((N_GROUPS, t), jnp.bool_)
    for _ in range(TOPK_GROUPS):
        m = jnp.max(gwork, axis=0, keepdims=True)
        ii = jnp.min(jnp.where(gwork == m, iota8, N_GROUPS), axis=0, keepdims=True)
        hit = iota8 == ii
        gsel = gsel | hit
        gwork = jnp.where(hit, neg, gwork)
    emask = jnp.concatenate(
        [jnp.broadcast_to(gsel[g:g + 1], (GROUP_SIZE, t)) for g in range(N_GROUPS)], axis=0)

    iota_e = lax.broadcasted_iota(I32, (N_EXPERTS, t), 0)
    work = jnp.where(emask, choice, neg)
    sel = jnp.zeros((N_EXPERTS, t), jnp.bool_)
    idx_rows, raw_rows = [], []
    for _ in range(TOP_K):
        m = jnp.max(work, axis=0, keepdims=True)
        ii = jnp.min(jnp.where(work == m, iota_e, N_EXPERTS), axis=0, keepdims=True)
        hit = iota_e == ii
        idx_rows.append(ii)
        raw_rows.append(jnp.sum(jnp.where(hit, scores, 0.0), axis=0, keepdims=True))
        sel = sel | hit
        work = jnp.where(hit, neg, work)
    wsum = raw_rows[0]
    for r in raw_rows[1:]:
        wsum = wsum + r
    scale = ROUTED_SCALE / (wsum + 1e-20)

    dense = jnp.zeros((N_EXPERTS, t), F32)
    for k in range(TOP_K):
        dense = jnp.where(iota_e == idx_rows[k], raw_rows[k] * scale, dense)
    wdense_ref[...] = jnp.concatenate([dense, jnp.zeros((LANES - N_EXPERTS, t), F32)], axis=0).T

    upper = (lax.broadcasted_iota(I32, (t, t), 0) < lax.broadcasted_iota(I32, (t, t), 1))
    sel_f = sel.astype(F32)
    prefix = _dot(sel_f.astype(BF16), upper.astype(F32).astype(BF16)) + carry_ref[...]
    pos_rows = [jnp.sum(jnp.where(iota_e == idx_rows[k], prefix, 0.0), axis=0, keepdims=True)
                for k in range(TOP_K)]
    dest_ref[step] = jnp.concatenate(pos_rows, axis=0).astype(I32)
    idx_ref[step] = jnp.concatenate(idx_rows, axis=0)
    carry = carry_ref[...] + jnp.sum(sel_f, axis=1, keepdims=True)
    carry_ref[...] = carry

    @pl.when(step == pl.num_programs(0) - 1)
    def _():
        cnt = carry.astype(I32)
        nblk = (cnt + (BM_EXP - 1)) >> BM_SHIFT
        lower = (lax.broadcasted_iota(I32, (N_EXPERTS, N_EXPERTS), 0)
                 >= lax.broadcasted_iota(I32, (N_EXPERTS, N_EXPERTS), 1))
        nblk_wide = jnp.broadcast_to(nblk.astype(F32), (N_EXPERTS, LANES)).astype(BF16)
        bend = _dot(lower.astype(F32).astype(BF16), nblk_wide)[:, 0:1].astype(I32)
        bstart = bend - nblk
        pad_start = bstart << BM_SHIFT
        pad_rows = jnp.broadcast_to(pad_start, (N_EXPERTS, t))

        def add_start(tile, c):
            idx_t = idx_ref[tile]
            add = jnp.zeros((TOP_K, t), I32)
            for ex in range(N_EXPERTS):
                add = jnp.where(idx_t == ex, pad_rows[ex:ex + 1], add)
            dest_ref[tile] = dest_ref[tile] + add
            return c

        lax.fori_loop(0, dest_ref.shape[0], add_start, 0)

        eye = (lax.broadcasted_iota(I32, (N_EXPERTS, LANES), 0) == lax.broadcasted_iota(I32, (N_EXPERTS, LANES), 1))

        def as_row(col):
            return jnp.sum(jnp.where(eye, col, 0), axis=0, keepdims=True)

        emeta_ref[...] = jnp.concatenate(
            [as_row(pad_start + cnt), as_row(bend << BM_SHIFT), as_row(bstart), as_row(nblk),
             jnp.zeros((SUBLANES - 4, LANES), I32)], axis=0)


def _router(x1, w_router_t, bias, l):
    n = x1.shape[0]
    nt = n // T_ROUTE
    return pl.pallas_call(
        _router_kernel,
        grid=(nt,),
        in_specs=[
            pl.BlockSpec((T_ROUTE, D_MODEL), lambda t: (t, 0)),
            pl.BlockSpec((None, N_EXPERTS, D_MODEL), lambda t: (l, 0, 0)),
            pl.BlockSpec((None, N_EXPERTS, 1), lambda t: (l, 0, 0)),
        ],
        out_specs=[
            pl.BlockSpec((nt, TOP_K, T_ROUTE), lambda t: (0, 0, 0)),
            pl.BlockSpec((T_ROUTE, LANES), lambda t: (t, 0)),
            pl.BlockSpec((SUBLANES, LANES), lambda t: (0, 0)),
        ],
        out_shape=[
            jax.ShapeDtypeStruct((nt, TOP_K, T_ROUTE), I32),
            jax.ShapeDtypeStruct((n, LANES), F32),
            jax.ShapeDtypeStruct((SUBLANES, LANES), I32),
        ],
        scratch_shapes=[pltpu.VMEM((N_EXPERTS, 1), F32), pltpu.VMEM((nt, TOP_K, T_ROUTE), I32)],
        compiler_params=pltpu.CompilerParams(vmem_limit_bytes=VMEM_LIMIT),
        name="router",
    )(x1, w_router_t, bias)


def _slot_kernel(dest_ref, emeta_ref, off_ref, *, n_tok):
    spare = n_tok * SUBLANES
    group = SUBLANES

    def fill_range(start, end):
        def fill(j, c):
            for u in range(group):
                off_ref[start + j * group + u] = spare
            return c
        lax.fori_loop(0, (end - start + group - 1) // group, fill, 0)

    def fill_expert(ex, c):
        fill_range(emeta_ref[0, ex], emeta_ref[1, ex])
        return c

    lax.fori_loop(0, N_EXPERTS, fill_expert, 0)
    fill_range(emeta_ref[1, N_EXPERTS - 1], off_ref.shape[0])

    def tile_body(tile, c):
        for k in range(TOP_K):
            base = (tile * TOP_K + k) * T_ROUTE

            def inner(j, c2, base=base):
                src = base + j * SLOT_UNROLL
                val = (tile * T_ROUTE + j * SLOT_UNROLL) * SUBLANES
                for u in range(SLOT_UNROLL):
                    off_ref[dest_ref[src + u]] = val + u * SUBLANES
                return c2

            lax.fori_loop(0, T_ROUTE // SLOT_UNROLL, inner, 0)
        return c

    lax.fori_loop(0, n_tok // T_ROUTE, tile_body, 0)


def _slot_table(dest_flat, emeta, n, nb):
    smem = pl.BlockSpec(memory_space=pltpu.SMEM)
    return pl.pallas_call(
        functools.partial(_slot_kernel, n_tok=n),
        in_specs=[smem, smem],
        out_specs=smem,
        out_shape=jax.ShapeDtypeStruct(((nb + 1) * BM_EXP,), I32),
        name="slot_table",
    )(dest_flat, emeta)


def _block_rows(first_block, j):
    return pl.ds(pl.multiple_of((first_block + j) * BM_EXP, BM_EXP), BM_EXP)


def _gather_rows(g, off_ref, xpk_ref, wdense_ref, stage_ref, wrow_ref):
    row_mask = xpk_ref.shape[0] - 1
    base = g * BM_EXP
    for r in range(BM_EXP):
        off = off_ref[base + r] & row_mask
        stage_ref[r * SUBLANES:(r + 1) * SUBLANES, :] = xpk_ref[pl.ds(pl.multiple_of(off, SUBLANES), SUBLANES), :]
        wrow_ref[r:r + 1, :] = wdense_ref[pl.ds(off >> 3, 1), :]


def _e1_kernel(bstart_ref, nblk_ref, off_ref, xpk_ref, wdense_ref, wg_ref, wu_ref, h_hbm,
               stage_a, stage_b, wrow_a, wrow_b, lhs_ref, wcat_ref, hbuf_ref, sem):
    e = pl.program_id(0)
    nblk = nblk_ref[e]
    first = bstart_ref[e]
    stages = ((stage_a, wrow_a), (stage_b, wrow_b))

    def h_copy(g):
        return pltpu.make_async_copy(hbuf_ref.at[g & 1], h_hbm.at[_block_rows(0, g)], sem.at[g & 1])

    @pl.when(e == 0)
    def _():
        _gather_rows(0, off_ref, xpk_ref, wdense_ref, stage_a, wrow_a)

    @pl.when(nblk > 0)
    def _():
        wcat_ref[:, :D_EXPERT] = wg_ref[...].astype(BF16)
        wcat_ref[:, D_EXPERT:] = wu_ref[...].astype(BF16)
        lane = lax.broadcasted_iota(I32, (BM_EXP, LANES), 1)

        def compute(g, parity):
            stage_ref, wrow_ref = stages[parity]
            next_stage, next_wrow = stages[1 - parity]
            wcol = jnp.sum(jnp.where(lane == e, wrow_ref[...], 0.0), axis=1, keepdims=True)
            for r0 in range(0, BM_EXP, UNPACK_ROWS):
                for s in range(SUBLANES):
                    lo, hi = _unpack_tokens(stage_ref[pl.ds(r0 * SUBLANES + s, UNPACK_ROWS, stride=SUBLANES), :])
                    lhs_ref[r0:r0 + UNPACK_ROWS, s * LANES:(s + 1) * LANES] = lo
                    lhs_ref[r0:r0 + UNPACK_ROWS, HALF + s * LANES:HALF + (s + 1) * LANES] = hi
            gu = _dot(lhs_ref[...], wcat_ref[...])
            anchor = jnp.max(jnp.max(next_wrow[...], axis=0, keepdims=True), axis=1, keepdims=True) * 0.0
            hbuf_ref[parity] = (jax.nn.silu(gu[:, :D_EXPERT]) * gu[:, D_EXPERT:] * (wcol + anchor)).astype(BF16)
            h_copy(g).start()

        def block(j, carry):
            g = first + j

            @pl.when(g >= 2)
            def _():
                h_copy(g - 2).wait()

            for parity in (0, 1):
                @pl.when((g & 1) == parity)
                def _(parity=parity):
                    _gather_rows(g + 1, off_ref, xpk_ref, wdense_ref, *stages[1 - parity])
                    compute(g, parity)
            return carry

        lax.fori_loop(0, nblk, block, 0)

    @pl.when(e == N_EXPERTS - 1)
    def _():
        total = first + nblk

        @pl.when(total >= 2)
        def _():
            h_copy(total - 2).wait()

        @pl.when(total >= 1)
        def _():
            h_copy(total - 1).wait()

        hbuf_ref[0] = jnp.zeros((BM_EXP, D_EXPERT), BF16)

        def zero_block(j, carry):
            copy = pltpu.make_async_copy(hbuf_ref.at[0], h_hbm.at[_block_rows(0, j)], sem.at[0])
            copy.start()
            copy.wait()
            return carry

        lax.fori_loop(first + nblk, h_hbm.shape[0] // BM_EXP, zero_block, 0)


def _expert_gate_up(bstart, nblk, slot_off, xpk, wdense, exp_w_gate, exp_w_up, l):
    assert xpk.shape[0] & (xpk.shape[0] - 1) == 0
    grid_spec = pltpu.PrefetchScalarGridSpec(
        num_scalar_prefetch=3,
        grid=(N_EXPERTS,),
        in_specs=[
            pl.BlockSpec(xpk.shape, lambda e, *_: (0, 0), pipeline_mode=pl.Buffered(1)),
            pl.BlockSpec(wdense.shape, lambda e, *_: (0, 0), pipeline_mode=pl.Buffered(1)),
            pl.BlockSpec((None, None, D_MODEL, D_EXPERT), lambda e, *_: (l, e, 0, 0)),
            pl.BlockSpec((None, None, D_MODEL, D_EXPERT), lambda e, *_: (l, e, 0, 0)),
        ],
        out_specs=pl.BlockSpec(memory_space=pl.ANY),
        scratch_shapes=[
            pltpu.VMEM((BM_EXP * SUBLANES, LANES), U32),
            pltpu.VMEM((BM_EXP * SUBLANES, LANES), U32),
            pltpu.VMEM((BM_EXP, LANES), F32),
            pltpu.VMEM((BM_EXP, LANES), F32),
            pltpu.VMEM((BM_EXP, D_MODEL), BF16),
            pltpu.VMEM((D_MODEL, 2 * D_EXPERT), BF16),
            pltpu.VMEM((2, BM_EXP, D_EXPERT), BF16),
            pltpu.SemaphoreType.DMA((2,)),
        ],
    )
    return pl.pallas_call(
        _e1_kernel,
        grid_spec=grid_spec,
        out_shape=jax.ShapeDtypeStruct((slot_off.shape[0], D_EXPERT), BF16),
        compiler_params=pltpu.CompilerParams(vmem_limit_bytes=VMEM_LIMIT),
        name="expert_gate_up",
    )(bstart, nblk, slot_off, xpk, wdense, exp_w_gate, exp_w_up)


def _e2_kernel(bstart_ref, nblk_ref, off_ref, h_hbm, wd_ref, out_ref, ybuf_a, ybuf_b, hbuf_ref, wdb_ref, sem):
    e = pl.program_id(1)
    nblk = nblk_ref[e]
    first = bstart_ref[e]
    total = bstart_ref[N_EXPERTS - 1] + nblk_ref[N_EXPERTS - 1]
    group = SUBLANES
    chunks = HALF // LANES
    ybufs = (ybuf_a, ybuf_b)

    def h_copy(g):
        slot = g % H_SLOTS
        return pltpu.make_async_copy(h_hbm.at[_block_rows(0, g)], hbuf_ref.at[slot], sem.at[slot])

    def project(g, ybuf_ref):
        y = _dot(hbuf_ref[g % H_SLOTS], wdb_ref[...])
        for t in range(BM_EXP // SUBLANES):
            for c in range(chunks):
                t0 = (t * chunks + c) * SUBLANES
                ybuf_ref[t0:t0 + SUBLANES, :] = y[t * SUBLANES:(t + 1) * SUBLANES, c * LANES:(c + 1) * LANES]

    def scatter(g, ybuf_ref):
        base = g * BM_EXP
        for g0 in range(0, BM_EXP, group):
            dsts, vals = [], []
            for r in range(g0, g0 + group):
                dst = pl.ds(pl.multiple_of(off_ref[base + r], SUBLANES), SUBLANES)
                row = ybuf_ref[pl.ds((r // SUBLANES) * chunks * SUBLANES + r % SUBLANES, chunks, stride=SUBLANES), :]
                dsts.append(dst)
                vals.append(out_ref[dst, :] + row)
            for dst, val in zip(dsts, vals):
                out_ref[dst, :] = val

    @pl.when(e == 0)
    def _():
        out_ref[...] = jnp.zeros_like(out_ref)
        for g in range(H_SLOTS):
            @pl.when(g <= total)
            def _(g=g):
                h_copy(g).start()

    @pl.when(nblk > 0)
    def _():
        wdb_ref[...] = wd_ref[...].astype(BF16)

        @pl.when(first == 0)
        def _():
            h_copy(0).wait()

        for parity in (0, 1):
            @pl.when((first & 1) == parity)
            def _(parity=parity):
                project(first, ybufs[parity])

        def block(j, carry):
            g = first + j
            h_copy(g + 1).wait()

            for parity in (0, 1):
                @pl.when((g & 1) == parity)
                def _(parity=parity):
                    project(g + 1, ybufs[1 - parity])
                    scatter(g, ybufs[parity])

            @pl.when(g + H_SLOTS <= total)
            def _():
                h_copy(g + H_SLOTS).start()
            return carry

        lax.fori_loop(0, nblk, block, 0)


def _expert_down_combine(bstart, nblk, slot_off, hmid, exp_w_down, l, n):
    rows = (n + 1) * SUBLANES
    grid_spec = pltpu.PrefetchScalarGridSpec(
        num_scalar_prefetch=3,
        grid=(D_MODEL // HALF, N_EXPERTS),
        in_specs=[
            pl.BlockSpec(memory_space=pl.ANY),
            pl.BlockSpec((None, None, D_EXPERT, HALF), lambda c, e, *_: (l, e, 0, c)),
        ],
        out_specs=pl.BlockSpec((None, rows, LANES), lambda c, e, *_: (c, 0, 0),
                               pipeline_mode=pl.Buffered(1)),
        scratch_shapes=[
            pltpu.VMEM((BM_EXP * HALF // LANES, LANES), F32),
            pltpu.VMEM((BM_EXP * HALF // LANES, LANES), F32),
            pltpu.VMEM((H_SLOTS, BM_EXP, D_EXPERT), BF16),
            pltpu.VMEM((D_EXPERT, HALF), BF16),
            pltpu.SemaphoreType.DMA((H_SLOTS,)),
        ],
    )
    return pl.pallas_call(
        _e2_kernel,
        grid_spec=grid_spec,
        out_shape=jax.ShapeDtypeStruct((D_MODEL // HALF, rows, LANES), F32),
        compiler_params=pltpu.CompilerParams(vmem_limit_bytes=VMEM_LIMIT),
        name="expert_down_combine",
    )(bstart, nblk, slot_off, hmid, exp_w_down)


def _final_kernel(x1_ref, routed_ref, wgu_ref, wd_ref, g_ref, b_ref, x2_ref, x2bf_ref):
    ts = x1_ref.shape[0]
    gu = _dot(x1_ref[...].astype(BF16), wgu_ref[...])
    hs = (jax.nn.silu(gu[:, :D_SHARED]) * gu[:, D_SHARED:]).astype(BF16)
    shared = _dot(hs, wd_ref[...])
    routed = jnp.concatenate(
        [routed_ref[c, pl.ds(s, ts, stride=SUBLANES), :]
         for c in range(D_MODEL // HALF) for s in range(SUBLANES)], axis=1)
    z = ALPHA * x1_ref[...] + (shared + routed)
    x2 = _ln(z, g_ref[...], b_ref[...])
    x2_ref[...] = x2
    x2bf_ref[...] = x2.astype(BF16)


def _shared_ln2(x1, routed, sh_gu_bf, sh_d_bf, ln2_g, ln2_b, l):
    n = x1.shape[0]
    row_spec = pl.BlockSpec((TS_FIN, D_MODEL), lambda i: (i, 0))
    return pl.pallas_call(
        _final_kernel,
        grid=(n // TS_FIN,),
        in_specs=[
            row_spec,
            pl.BlockSpec((D_MODEL // HALF, TS_FIN * SUBLANES, LANES), lambda i: (0, i, 0)),
            pl.BlockSpec((None, D_MODEL, 2 * D_SHARED), lambda i: (l, 0, 0)),
            pl.BlockSpec((None, D_SHARED, D_MODEL), lambda i: (l, 0, 0)),
            pl.BlockSpec((None, 1, D_MODEL), lambda i: (l, 0, 0)),
            pl.BlockSpec((None, 1, D_MODEL), lambda i: (l, 0, 0)),
        ],
        out_specs=[row_spec, row_spec],
        out_shape=[jax.ShapeDtypeStruct((n, D_MODEL), F32), jax.ShapeDtypeStruct((n, D_MODEL), BF16)],
        compiler_params=pltpu.CompilerParams(vmem_limit_bytes=VMEM_LIMIT),
        name="shared_ln2",
    )(x1, routed, sh_gu_bf, sh_d_bf, ln2_g, ln2_b)


def kernel(x, mem, mem_ln_g, mem_ln_b, w_in, sgu_ln_g, sgu_ln_b, sgu_w, sgu_b, pool_w, pool_scale, conv_w, conv_b, conv_ln_g, conv_ln_b, w_mem_kv, w_out, ln1_g, ln1_b, w_router, router_bias, exp_w_gate, exp_w_up, exp_w_down, sh_w_gate, sh_w_up, sh_w_down, ln2_g, ln2_b):
    batch, seq, d = x.shape
    n = batch * seq
    depth = w_in.shape[0]
    assert (d, depth) == (D_MODEL, DEPTH) and seq % TS_MIX == 0 and n % BM_IN == 0 and n % T_ROUTE == 0

    w_in_p = jnp.concatenate([w_in[..., :2 * GW], w_in[..., 5 * GW:], w_in[..., 2 * GW:5 * GW]], axis=-1).astype(BF16)
    wout_bf = w_out.astype(BF16)
    sh_gu_bf = jnp.concatenate([sh_w_gate, sh_w_up], axis=-1).astype(BF16)
    sh_d_bf = sh_w_down.astype(BF16)
    w_router_t = jnp.swapaxes(w_router, 1, 2)
    bias3 = router_bias[..., None]

    def row(a):
        return a[:, None, :]

    p = dict(sgu_w=sgu_w, sgu_b=sgu_b, sgu_ln_g=row(sgu_ln_g), sgu_ln_b=row(sgu_ln_b), pool_w=pool_w,
             pool_scale=row(pool_scale), conv_w=conv_w, conv_b=row(conv_b), conv_ln_g=row(conv_ln_g),
             conv_ln_b=row(conv_ln_b), ln1_g=row(ln1_g), ln1_b=row(ln1_b))
    ln2_g3, ln2_b3 = row(ln2_g), row(ln2_b)

    kv = _memory_kv(mem.reshape(batch * N_MEM, d), mem_ln_g[None, :], mem_ln_b[None, :], w_mem_kv)

    nb = n * TOP_K // BM_EXP + N_EXPERTS
    assert (1 << BM_SHIFT) == BM_EXP

    xf = x.reshape(n, d)
    xbf = xf.astype(BF16)
    for l in range(depth):
        h6 = _in_proj(xbf, w_in_p, l)
        x1, xpk = _mixers(h6, kv, xf, wout_bf, p, l, batch, seq)
        dest, wdense, emeta = _router(x1, w_router_t, bias3, l)
        slot_off = _slot_table(dest.reshape(-1), emeta, n, nb)
        bstart, nblk = emeta[2, :N_EXPERTS], emeta[3, :N_EXPERTS]
        hmid = _expert_gate_up(bstart, nblk, slot_off, xpk, wdense, exp_w_gate, exp_w_up, l)
        routed = _expert_down_combine(bstart, nblk, slot_off, hmid, exp_w_down, l, n)
        xf, xbf = _shared_ln2(x1, routed, sh_gu_bf, sh_d_bf, ln2_g3, ln2_b3, l)
    return xf.reshape(batch, seq, d)
```
